```python
import math
import jax, jax.numpy as jnp
from jax import lax
import numpy as np

D_MODEL = 1024
BATCH = 8
SEQ = 2048
DEPTH = 1
DEC_BATCH = 128
DEC_SEQ = 8
PAST_LEN = 8192
PAGE_SIZE = 128

DA_WIDTH = D_MODEL // 2
LRU_WIDTH = D_MODEL - DA_WIDTH
MIX_WIDTH = DA_WIDTH + LRU_WIDTH
N_DA_HEADS = 4
DV = DA_WIDTH // N_DA_HEADS
DQK = DV // 2
N_LRU_HEADS = 8
LRU_BLOCK = LRU_WIDTH // N_LRU_HEADS
CONV_WIDTH = 4
LRU_C = 8.0
D_FF = 4 * D_MODEL
Q_BLOCK = 128
IN_WIDTH = 3 * DA_WIDTH + 2 * LRU_WIDTH
EPS = 1e-6

kernel_name = "hymba_diffattn_rglru_decoder_step"


def rmsnorm(x, g):
    xf = x.astype(jnp.float32)
    y = xf * lax.rsqrt(jnp.mean(xf * xf, axis=-1, keepdims=True) + EPS)
    return (y * g.astype(jnp.float32)).astype(x.dtype)


def alibi_slopes():
    return 2.0 ** (-8.0 * jnp.arange(1, N_DA_HEADS + 1, dtype=jnp.float32) / N_DA_HEADS)


def lambda_init(layer):
    return 0.8 - 0.6 * math.exp(-0.3 * layer)


def diff_attn_core(q, k, v, q_pos, k_pos, lam):
    s = jnp.einsum('bqhcd,bkhcd->bhcqk', q, k).astype(jnp.float32) * (DQK ** -0.5)
    dist = (q_pos[:, None] - k_pos[None, :]).astype(jnp.float32)
    bias = jnp.where(dist >= 0, -alibi_slopes()[:, None, None] * dist, -jnp.inf)
    p = jax.nn.softmax(s + bias[None, :, None], axis=-1)
    a = p[:, :, 0] - lam * p[:, :, 1]
    return jnp.einsum('bhqk,bkhe->bqhe', a.astype(v.dtype), v)


def prompt_attention(q, k, v, lam):
    b, s = q.shape[0], q.shape[1]
    nb = s // Q_BLOCK
    qb = q.reshape(b, nb, Q_BLOCK, N_DA_HEADS, 2, DQK).transpose(1, 0, 2, 3, 4, 5)
    k_pos = jnp.arange(s)

    def one_block(args):
        q_blk, i = args
        q_pos = i * Q_BLOCK + jnp.arange(Q_BLOCK)
        return diff_attn_core(q_blk, k, v, q_pos, k_pos, lam)

    o = lax.map(one_block, (qb, jnp.arange(nb)))
    return o.transpose(1, 0, 2, 3, 4).reshape(b, s, N_DA_HEADS, DV)


def sample_attention(q, k_new, v_new, cache_k, cache_v, page_table, layer, lam):
    n_pages = page_table.shape[1]
    past = n_pages * PAGE_SIZE
    t = q.shape[1]
    q_pos = past + jnp.arange(t)
    k_pos = jnp.arange(past + t)

    def one_seq(args):
        q_i, kn, vn, pt = args
        kp = cache_k[layer, pt].reshape(past, N_DA_HEADS, 2, DQK)
        vp = cache_v[layer, pt].reshape(past, N_DA_HEADS, DV)
        kk = jnp.concatenate([kp.astype(kn.dtype), kn], axis=0)[None]
        vv = jnp.concatenate([vp.astype(vn.dtype), vn], axis=0)[None]
        return diff_attn_core(q_i[None], kk, vv, q_pos, k_pos, lam)[0]

    return lax.map(one_seq, (q, k_new, v_new, page_table))


def causal_conv(x, buf, w, b):
    xp = jnp.concatenate([buf.astype(x.dtype), x], axis=1)
    t = x.shape[1]
    y = b
    for j in range(CONV_WIDTH):
        y = y + xp[:, j:j + t] * w[j]
    return y, xp[:, xp.shape[1] - (CONV_WIDTH - 1):]


def rg_lru(x, h0, w_rg, b_rg, w_ig, b_ig, lru_lambda):
    bsz, t, w = x.shape
    xb = x.reshape(bsz, t, N_LRU_HEADS, LRU_BLOCK)
    r = jax.nn.sigmoid(jnp.einsum('bthi,hij->bthj', xb, w_rg).reshape(bsz, t, w) + b_rg)
    ig = jax.nn.sigmoid(jnp.einsum('bthi,hij->bthj', xb, w_ig).reshape(bsz, t, w) + b_ig)
    log_a = -LRU_C * r.astype(jnp.float32) * jax.nn.softplus(-lru_lambda.astype(jnp.float32))
    a = jnp.exp(log_a)
    u = jnp.sqrt(-jnp.expm1(2.0 * log_a)) * (ig * x).astype(jnp.float32)

    def step(h, au):
        a_t, u_t = au
        h = a_t * h + u_t
        return h, h

    h_last, hs = lax.scan(step, h0.astype(jnp.float32),
                          (a.transpose(1, 0, 2), u.transpose(1, 0, 2)))
    return hs.transpose(1, 0, 2).astype(x.dtype), h_last.astype(x.dtype)


def decoder_layer(x, c, attend, conv_buf, h0, lam_init,
                  w_ada, b_ada, g_norm1, g_norm2, w_in, lq1, lk1, lq2, lk2, g_subln,
                  conv_w, conv_b, w_rg, b_rg, w_ig, b_ig, lru_lambda, w_out, w_ff1, w_ff2):
    bsz, t = x.shape[0], x.shape[1]
    mod = jax.nn.silu(c) @ w_ada + b_ada
    sh1, sc1, gt1, sh2, sc2, gt2 = jnp.split(mod[:, None, :], 6, axis=-1)
    h = rmsnorm(x, g_norm1) * (1.0 + sc1) + sh1
    proj = h @ w_in
    q, k, v, xl, gl = jnp.split(
        proj, [DA_WIDTH, 2 * DA_WIDTH, 3 * DA_WIDTH, 3 * DA_WIDTH + LRU_WIDTH], axis=-1)
    q = q.reshape(bsz, t, N_DA_HEADS, 2, DQK)
    k = k.reshape(bsz, t, N_DA_HEADS, 2, DQK)
    v = v.reshape(bsz, t, N_DA_HEADS, DV)
    lam = (jnp.exp(jnp.sum(lq1.astype(jnp.float32) * lk1.astype(jnp.float32)))
           - jnp.exp(jnp.sum(lq2.astype(jnp.float32) * lk2.astype(jnp.float32))) + lam_init)
    o = attend(q, k, v, lam)
    o = rmsnorm(o, g_subln) * (1.0 - lam_init)
    xc, new_conv = causal_conv(xl, conv_buf, conv_w, conv_b)
    hs, h_last = rg_lru(xc, h0, w_rg, b_rg, w_ig, b_ig, lru_lambda)
    y_lru = hs * jax.nn.gelu(gl)
    mix = jnp.concatenate([o.reshape(bsz, t, DA_WIDTH), y_lru], axis=-1) @ w_out
    x = x + gt1 * mix
    h2 = rmsnorm(x, g_norm2) * (1.0 + sc2) + sh2
    ff = jnp.square(jax.nn.relu(h2 @ w_ff1)) @ w_ff2
    x = x + gt2 * ff
    return x, k.reshape(bsz, t, N_DA_HEADS, 2 * DQK), v, new_conv, h_last


def setup_inputs(seed: int = 0) -> dict:
    key = jax.random.key(seed)
    ks = jax.random.split(key, 40)
    n_pages = PAST_LEN // PAGE_SIZE
    n_used = DEC_BATCH * n_pages
    n_pool = (n_used * 5) // 4
    f32 = jnp.float32

    def nrm(k, shape, scale=1.0):
        return jax.random.normal(k, shape, f32) * scale

    a0 = jax.random.uniform(ks[20], (DEPTH, LRU_WIDTH), f32, 0.9, 0.999)
    page_table = jax.random.permutation(ks[6], n_pool)[:n_used].reshape(DEC_BATCH, n_pages).astype(jnp.int32)
    return {
        "x_prompt": nrm(ks[0], (BATCH, SEQ, D_MODEL)),
        "x_sample": nrm(ks[1], (DEC_BATCH, DEC_SEQ, D_MODEL)),
        "c_prompt": nrm(ks[2], (BATCH, D_MODEL)),
        "c_sample": nrm(ks[3], (DEC_BATCH, D_MODEL)),
        "cache_k": nrm(ks[4], (DEPTH, n_pool, PAGE_SIZE, N_DA_HEADS, 2 * DQK)),
        "cache_v": nrm(ks[5], (DEPTH, n_pool, PAGE_SIZE, N_DA_HEADS, DV)),
        "page_table": page_table,
        "state_h": nrm(ks[7], (DEPTH, DEC_BATCH, LRU_WIDTH), 0.5),
        "state_conv": nrm(ks[8], (DEPTH, DEC_BATCH, CONV_WIDTH - 1, LRU_WIDTH)),
        "w_ada": nrm(ks[9], (DEPTH, D_MODEL, 6 * D_MODEL), D_MODEL ** -0.5),
        "b_ada": nrm(ks[10], (DEPTH, 6 * D_MODEL), 0.01),
        "g_norm1": 1.0 + nrm(ks[11], (DEPTH, D_MODEL), 0.02),
        "g_norm2": 1.0 + nrm(ks[12], (DEPTH, D_MODEL), 0.02),
        "w_in": nrm(ks[13], (DEPTH, D_MODEL, IN_WIDTH), D_MODEL ** -0.5),
        "lambda_q1": nrm(ks[14], (DEPTH, DQK), 0.1),
        "lambda_k1": nrm(ks[15], (DEPTH, DQK), 0.1),
        "lambda_q2": nrm(ks[16], (DEPTH, DQK), 0.1),
        "lambda_k2": nrm(ks[17], (DEPTH, DQK), 0.1),
        "g_subln": 1.0 + nrm(ks[18], (DEPTH, DV), 0.02),
        "conv_w": nrm(ks[19], (DEPTH, CONV_WIDTH, LRU_WIDTH), CONV_WIDTH ** -0.5),
        "conv_b": nrm(ks[21], (DEPTH, LRU_WIDTH), 0.01),
        "w_rg": nrm(ks[22], (DEPTH, N_LRU_HEADS, LRU_BLOCK, LRU_BLOCK), LRU_BLOCK ** -0.5),
        "b_rg": nrm(ks[23], (DEPTH, LRU_WIDTH), 0.01),
        "w_ig": nrm(ks[24], (DEPTH, N_LRU_HEADS, LRU_BLOCK, LRU_BLOCK), LRU_BLOCK ** -0.5),
        "b_ig": nrm(ks[25], (DEPTH, LRU_WIDTH), 0.01),
        "lru_lambda": jnp.log(a0) - jnp.log1p(-a0),
        "w_out": nrm(ks[26], (DEPTH, MIX_WIDTH, D_MODEL), MIX_WIDTH ** -0.5),
        "w_ff1": nrm(ks[27], (DEPTH, D_MODEL, D_FF), D_MODEL ** -0.5),
        "w_ff2": nrm(ks[28], (DEPTH, D_FF, D_MODEL), D_FF ** -0.5),
        "g_final": 1.0 + nrm(ks[29], (D_MODEL,), 0.02),
    }


def reference(x_prompt, x_sample, c_prompt, c_sample, cache_k, cache_v, page_table,
              state_h, state_conv, w_ada, b_ada, g_norm1, g_norm2, w_in,
              lambda_q1, lambda_k1, lambda_q2, lambda_k2, g_subln, conv_w, conv_b,
              w_rg, b_rg, w_ig, b_ig, lru_lambda, w_out, w_ff1, w_ff2, g_final):
    yp, ys = x_prompt, x_sample
    kp_l, vp_l, cp_l, hp_l, ks_l, vs_l, cs_l, hs_l = [], [], [], [], [], [], [], []
    for l in range(DEPTH):
        lam0 = lambda_init(l)
        lw = (w_ada[l], b_ada[l], g_norm1[l], g_norm2[l], w_in[l], lambda_q1[l], lambda_k1[l],
              lambda_q2[l], lambda_k2[l], g_subln[l], conv_w[l], conv_b[l], w_rg[l], b_rg[l],
              w_ig[l], b_ig[l], lru_lambda[l], w_out[l], w_ff1[l], w_ff2[l])
        conv0 = jnp.zeros((yp.shape[0], CONV_WIDTH - 1, LRU_WIDTH), yp.dtype)
        h0 = jnp.zeros((yp.shape[0], LRU_WIDTH), yp.dtype)
        yp, kp, vp, cp, hp = decoder_layer(yp, c_prompt, prompt_attention, conv0, h0, lam0, *lw)

        def attend_sample(q, k, v, lam, layer=l):
            return sample_attention(q, k, v, cache_k, cache_v, page_table, layer, lam)

        ys, ksm, vsm, csm, hsm = decoder_layer(ys, c_sample, attend_sample, state_conv[l],
                                               state_h[l], lam0, *lw)
        kp_l.append(kp); vp_l.append(vp); cp_l.append(cp); hp_l.append(hp)
        ks_l.append(ksm); vs_l.append(vsm); cs_l.append(csm); hs_l.append(hsm)
    y_prompt = rmsnorm(yp, g_final)
    y_sample = rmsnorm(ys, g_final)
    return (y_prompt, y_sample,
            jnp.stack(kp_l), jnp.stack(vp_l), jnp.stack(cp_l), jnp.stack(hp_l),
            jnp.stack(ks_l), jnp.stack(vs_l), jnp.stack(cs_l), jnp.stack(hs_l))
```

```python
import functools
import math

import jax
import jax.numpy as jnp
from jax import lax
from jax.experimental import pallas as pl
from jax.experimental.pallas import tpu as pltpu

F32 = jnp.float32
BF16 = jnp.bfloat16

N_DA_HEADS = 4
N_LRU_HEADS = 8
CONV_WIDTH = 4
LRU_C = 8.0
EPS = 1e-6
NEG_BIG = -1e30
SUBLANES = 8
LANES = 128
VMEM_LIMIT_BYTES = 56 * 1024 * 1024


def _lambda_init(layer):
    return 0.8 - 0.6 * math.exp(-0.3 * layer)


def _params(*sem):
    return pltpu.CompilerParams(dimension_semantics=sem, vmem_limit_bytes=VMEM_LIMIT_BYTES)


def _rms(x, g):
    return x * lax.rsqrt(jnp.mean(x * x, axis=-1, keepdims=True) + EPS) * g


def _ada_kernel(c_ref, w_ref, b_ref, o_ref):
    c = c_ref[...]
    s = (c * jax.nn.sigmoid(c)).astype(BF16)
    o_ref[...] = jnp.dot(s, w_ref[...].astype(BF16), preferred_element_type=F32) + b_ref[...]


def _ada_mod(c, w, b, tn=1536):
    m, d = c.shape
    n = w.shape[1]
    return pl.pallas_call(
        _ada_kernel,
        grid=(n // tn,),
        in_specs=[pl.BlockSpec((m, d), lambda j: (0, 0)),
                  pl.BlockSpec((d, tn), lambda j: (0, j)),
                  pl.BlockSpec((1, tn), lambda j: (0, j))],
        out_specs=pl.BlockSpec((m, tn), lambda j: (0, j)),
        out_shape=jax.ShapeDtypeStruct((m, n), F32),
        compiler_params=_params("arbitrary"),
        name="ada_mod",
    )(c, w, b.reshape(1, n))


def _inproj_kernel(x_ref, sc_ref, sh_ref, g_ref, w_ref,
                   q_ref, k_ref, v_ref, kb_ref, vb_ref, xl_ref, gl_ref, *, da_w, lru_w, q_scale):
    bb, tt, d = x_ref.shape
    h = _rms(x_ref[...], g_ref[...]) * (1.0 + sc_ref[...]) + sh_ref[...]
    h = h.reshape(bb * tt, d).astype(BF16)
    proj = jnp.dot(h, w_ref[...], preferred_element_type=F32)
    k = proj[:, da_w:2 * da_w]
    v = proj[:, 2 * da_w:3 * da_w]
    q_ref[...] = (proj[:, :da_w] * q_scale).astype(BF16)
    k_ref[...] = k
    v_ref[...] = v
    kb_ref[...] = k.astype(BF16)
    vb_ref[...] = v.astype(BF16)
    xl_ref[...] = proj[:, 3 * da_w:3 * da_w + lru_w]
    gl_ref[...] = proj[:, 3 * da_w + lru_w:]


def _in_proj(x, sc, sh, g, w_bf, bb, tt, da_w, lru_w):
    b, t, d = x.shape
    n_tok = b * t
    nt = t // tt
    rows = bb * tt
    dqk = da_w // N_DA_HEADS // 2

    def flat(width, dtype):
        return (pl.BlockSpec((rows, width), lambda i, j: (i * nt + j, 0)),
                jax.ShapeDtypeStruct((n_tok, width), dtype))

    outs = [flat(da_w, BF16), flat(da_w, F32), flat(da_w, F32), flat(da_w, BF16), flat(da_w, BF16),
            flat(lru_w, F32), flat(lru_w, F32)]
    return pl.pallas_call(
        functools.partial(_inproj_kernel, da_w=da_w, lru_w=lru_w, q_scale=dqk ** -0.5),
        grid=(b // bb, nt),
        in_specs=[pl.BlockSpec((bb, tt, d), lambda i, j: (i, j, 0)),
                  pl.BlockSpec((bb, 1, d), lambda i, j: (i, 0, 0)),
                  pl.BlockSpec((bb, 1, d), lambda i, j: (i, 0, 0)),
                  pl.BlockSpec((1, 1, d), lambda i, j: (0, 0, 0)),
                  pl.BlockSpec(w_bf.shape, lambda i, j: (0, 0))],
        out_specs=[o[0] for o in outs],
        out_shape=[o[1] for o in outs],
        compiler_params=_params("arbitrary", "arbitrary"),
        name="in_proj",
    )(x, sc, sh, g.reshape(1, 1, d), w_bf)


def _lam_value(lamp_ref, lam0):
    lp = lamp_ref[...]
    t1 = jnp.sum(lp[0:1] * lp[1:2], axis=-1, keepdims=True)
    t2 = jnp.sum(lp[2:3] * lp[3:4], axis=-1, keepdims=True)
    return jnp.exp(t1) - jnp.exp(t2) + lam0


def _softmax_update(s, m_sc, l_sc, acc_sc, pv_fn):
    m_old = m_sc[...]
    m_new = jnp.maximum(m_old, jnp.max(s, axis=-1, keepdims=True))
    alpha = jnp.exp(m_old - m_new)
    p = jnp.exp(s - m_new)
    l_sc[...] = alpha * l_sc[...] + jnp.sum(p, axis=-1, keepdims=True)
    acc_sc[...] = alpha * acc_sc[...] + pv_fn(p.astype(BF16))
    m_sc[...] = m_new


def _nt_dot(a, b):
    return lax.dot_general(a, b, (((1,), (1,)), ((), ())), preferred_element_type=F32)


def _pattn_kernel(slopes_ref, q_ref, k_ref, v_ref, lamp_ref, g_ref, o_ref, m_sc, l_sc, acc_sc,
                  *, tq, lam0):
    h = pl.program_id(1)
    qi = pl.program_id(2)
    dv = q_ref.shape[-1]
    slope = slopes_ref[h]

    q = q_ref[...]
    lane = lax.broadcasted_iota(jnp.int32, q.shape, 1)
    zero = jnp.zeros_like(q)
    qs = jnp.concatenate([jnp.where(lane < dv // 2, q, zero),
                          jnp.where(lane >= dv // 2, q, zero)], axis=0)

    m_sc[...] = jnp.full(m_sc.shape, NEG_BIG, F32)
    l_sc[...] = jnp.zeros(l_sc.shape, F32)
    acc_sc[...] = jnp.zeros(acc_sc.shape, F32)

    col = lax.broadcasted_iota(jnp.int32, (1, tq), 1)

    def logits(j):
        start = pl.multiple_of(j * tq, tq)
        k = k_ref[pl.ds(start, tq), :]
        rel = (col + (j - qi) * tq).astype(F32)
        return _nt_dot(qs, k) + slope * rel, start

    def body(j, carry):
        s, start = logits(j)
        v = v_ref[pl.ds(start, tq), :]
        _softmax_update(s, m_sc, l_sc, acc_sc,
                        lambda p: jnp.dot(p, v, preferred_element_type=F32))
        return carry

    lax.fori_loop(0, qi, body, 0)

    s, start = logits(qi)
    r = lax.broadcasted_iota(jnp.int32, (2 * tq, tq), 0)
    r = jnp.where(r >= tq, r - tq, r)
    c = lax.broadcasted_iota(jnp.int32, (2 * tq, tq), 1)
    s = jnp.where(c <= r, s, NEG_BIG)
    v = v_ref[pl.ds(start, tq), :]
    _softmax_update(s, m_sc, l_sc, acc_sc, lambda p: jnp.dot(p, v, preferred_element_type=F32))

    lam = _lam_value(lamp_ref, lam0)
    on = acc_sc[...] / l_sc[...]
    o = on[:tq] - lam * on[tq:]
    o_ref[...] = (_rms(o, g_ref[...]) * (1.0 - lam0)).astype(o_ref.dtype)


def _prompt_attention(q, kb, vb, lamp, g_subln, slopes, lam0, tq):
    b, s, da_w = q.shape
    dv = da_w // N_DA_HEADS
    kernel = functools.partial(_pattn_kernel, tq=tq, lam0=lam0)
    return pl.pallas_call(
        kernel,
        grid_spec=pltpu.PrefetchScalarGridSpec(
            num_scalar_prefetch=1,
            grid=(b, N_DA_HEADS, s // tq),
            in_specs=[pl.BlockSpec((None, tq, dv), lambda i, h, j, sl: (i, j, h)),
                      pl.BlockSpec((None, s, dv), lambda i, h, j, sl: (i, 0, h)),
                      pl.BlockSpec((None, s, dv), lambda i, h, j, sl: (i, 0, h)),
                      pl.BlockSpec(lamp.shape, lambda i, h, j, sl: (0, 0)),
                      pl.BlockSpec((1, dv), lambda i, h, j, sl: (0, 0))],
            out_specs=pl.BlockSpec((None, tq, dv), lambda i, h, j, sl: (i, j, h)),
            scratch_shapes=[pltpu.VMEM((2 * tq, 1), F32), pltpu.VMEM((2 * tq, 1), F32),
                            pltpu.VMEM((2 * tq, dv), F32)]),
        out_shape=jax.ShapeDtypeStruct((b, s, da_w), BF16),
        compiler_params=_params("arbitrary", "arbitrary", "arbitrary"),
        name="prompt_attn",
    )(slopes, q, kb, vb, lamp, g_subln.reshape(1, dv))


def _sattn_kernel(pt_ref, q_ref, kn_ref, vn_ref, lamp_ref, g_ref, *refs, n_pg, past, lam0, pool_off):
    del pt_ref, pool_off
    k_refs = refs[:n_pg]
    v_refs = refs[n_pg:2 * n_pg]
    o_ref, m_sc, l_sc, acc_sc, bias_sc = refs[2 * n_pg:]
    j = pl.program_id(1)
    nj = pl.num_programs(1)
    t_new, da_w = q_ref.shape
    dv = da_w // N_DA_HEADS
    rows_h = 2 * t_new
    n_rows = N_DA_HEADS * rows_h
    pg_cols = k_refs[0].shape[0]
    pg_keys = pg_cols // N_DA_HEADS

    row1 = lax.broadcasted_iota(jnp.int32, (n_rows, 1), 0)
    h_row = row1 // rows_h
    q_row = row1 % t_new
    slope_row = jnp.exp2(-8.0 * (h_row + 1).astype(F32) / N_DA_HEADS)

    qf = q_ref[...].astype(F32)
    lane = lax.broadcasted_iota(jnp.int32, (t_new, dv), 1)
    pieces = []
    for h in range(N_DA_HEADS):
        qh = qf[:, h * dv:(h + 1) * dv]
        pieces.append(jnp.where(lane < dv // 2, qh, 0.0))
        pieces.append(jnp.where(lane >= dv // 2, qh, 0.0))
    qall = jnp.concatenate(pieces, axis=0).astype(BF16)

    @pl.when(j == 0)
    def _init():
        m_sc[...] = jnp.full(m_sc.shape, NEG_BIG, F32)
        l_sc[...] = jnp.zeros(l_sc.shape, F32)
        acc_sc[...] = jnp.zeros(acc_sc.shape, F32)
        c = lax.broadcasted_iota(jnp.int32, bias_sc.shape, 1)
        key = (c // N_DA_HEADS).astype(F32)
        bias_sc[...] = jnp.where(c % N_DA_HEADS == h_row, slope_row * key, NEG_BIG)

    chunk_keys = n_pg * pg_keys
    off = slope_row * (j * chunk_keys - past).astype(F32)

    s = jnp.concatenate([_nt_dot(qall, k_refs[i][...].astype(BF16)) for i in range(n_pg)], axis=1)
    s = s + bias_sc[...] + off

    def pv(p):
        out = jnp.dot(p[:, :pg_cols], v_refs[0][...].astype(BF16), preferred_element_type=F32)
        for i in range(1, n_pg):
            out += jnp.dot(p[:, i * pg_cols:(i + 1) * pg_cols], v_refs[i][...].astype(BF16),
                           preferred_element_type=F32)
        return out

    _softmax_update(s, m_sc, l_sc, acc_sc, pv)

    @pl.when(j == nj - 1)
    def _finish():
        n_new = kn_ref.shape[0]
        pad = jnp.zeros((LANES - n_new, dv), F32)
        kn = jnp.concatenate([kn_ref[...], pad], axis=0).astype(BF16)
        vn = jnp.concatenate([vn_ref[...], pad], axis=0).astype(BF16)
        c = lax.broadcasted_iota(jnp.int32, (n_rows, LANES), 1)
        key = c // N_DA_HEADS
        ok = (c % N_DA_HEADS == h_row) & (key <= q_row) & (c < n_new)
        sn = jnp.where(ok, _nt_dot(qall, kn) + slope_row * key.astype(F32), NEG_BIG)
        _softmax_update(sn, m_sc, l_sc, acc_sc,
                        lambda p: jnp.dot(p, vn, preferred_element_type=F32))

        lam = _lam_value(lamp_ref, lam0)
        on = acc_sc[...] / l_sc[...]
        outs = []
        for h in range(N_DA_HEADS):
            o1 = on[h * rows_h:h * rows_h + t_new]
            o2 = on[h * rows_h + t_new:(h + 1) * rows_h]
            outs.append(_rms(o1 - lam * o2, g_ref[...]) * (1.0 - lam0))
        o_ref[...] = jnp.concatenate(outs, axis=1).astype(o_ref.dtype)


def _sample_attention(q, k_new, v_new, cache_k2, cache_v2, page_table, pool_off, lamp, g_subln,
                      lam0, n_pg):
    b, t_new, da_w = q.shape
    dv = da_w // N_DA_HEADS
    n_pages = page_table.shape[1]
    pg_cols = cache_k2.shape[1]
    past = n_pages * (pg_cols // N_DA_HEADS)
    n_rows = N_DA_HEADS * 2 * t_new
    kernel = functools.partial(_sattn_kernel, n_pg=n_pg, past=past, lam0=lam0, pool_off=pool_off)

    def page_spec(i):
        return pl.BlockSpec((None, pg_cols, dv),
                            lambda s, j, pt: (pt[s, j * n_pg + i] + pool_off, 0, 0))

    in_specs = ([pl.BlockSpec((None, t_new, da_w), lambda s, j, pt: (s, 0, 0)),
                 pl.BlockSpec((None, t_new * N_DA_HEADS, dv), lambda s, j, pt: (s, 0, 0)),
                 pl.BlockSpec((None, t_new * N_DA_HEADS, dv), lambda s, j, pt: (s, 0, 0)),
                 pl.BlockSpec(lamp.shape, lambda s, j, pt: (0, 0)),
                 pl.BlockSpec((1, dv), lambda s, j, pt: (0, 0))]
                + [page_spec(i) for i in range(n_pg)] + [page_spec(i) for i in range(n_pg)])
    return pl.pallas_call(
        kernel,
        grid_spec=pltpu.PrefetchScalarGridSpec(
            num_scalar_prefetch=1,
            grid=(b, n_pages // n_pg),
            in_specs=in_specs,
            out_specs=pl.BlockSpec((None, t_new, da_w), lambda s, j, pt: (s, 0, 0)),
            scratch_shapes=[pltpu.VMEM((n_rows, 1), F32), pltpu.VMEM((n_rows, 1), F32),
                            pltpu.VMEM((n_rows, dv), F32),
                            pltpu.VMEM((n_rows, n_pg * pg_cols), F32)]),
        out_shape=jax.ShapeDtypeStruct((b, t_new, da_w), BF16),
        compiler_params=_params("arbitrary", "arbitrary"),
        name="sample_attn",
    )(page_table, q, k_new, v_new, lamp, g_subln.reshape(1, dv),
      *([cache_k2] * n_pg), *([cache_v2] * n_pg))


def _softplus(z):
    return jnp.maximum(z, 0.0) + jnp.log1p(jnp.exp(-jnp.abs(z)))


def _gelu_tanh(x):
    return 0.5 * x * (1.0 + jnp.tanh(math.sqrt(2.0 / math.pi) * (x + 0.044715 * (x * x * x))))


def _lru_gates(xc, wg_ref, bg_ref, lam_ref):
    w = xc.shape[-1]
    g = jnp.dot(xc.astype(BF16), wg_ref[...], preferred_element_type=F32) + bg_ref[...]
    r = jax.nn.sigmoid(g[:, :w])
    ig = jax.nn.sigmoid(g[:, w:])
    log_a = -LRU_C * r * _softplus(-lam_ref[...])
    a = jnp.exp(log_a)
    u = jnp.sqrt(-jnp.tanh(log_a) * (1.0 + a * a)) * (ig * xc)
    return a, u


def _load_blocked(sc, rows):
    return jnp.concatenate([sc[c, rows, :] for c in range(sc.shape[0])], axis=1)


def _store_blocked(sc, rows, val):
    for c in range(sc.shape[0]):
        sc[c, rows, :] = val[:, c * LANES:(c + 1) * LANES]


def _blocked(rows, w):
    return pltpu.VMEM((w // LANES, rows, LANES), F32)


def _lru_prompt_kernel(xl_ref, gl_ref, cw_ref, cb_ref, wg_ref, bg_ref, lam_ref,
                       y_ref, conv_ref, hlast_ref, xbuf, a_sc, u_sc, hcar):
    t = pl.program_id(1)
    nt = pl.num_programs(1)
    tm, w = xl_ref.shape
    seg = tm // SUBLANES
    hist = CONV_WIDTH - 1

    @pl.when(t == 0)
    def _init():
        xbuf[0:SUBLANES, :] = jnp.zeros((SUBLANES, w), F32)
        hcar[...] = jnp.zeros(hcar.shape, F32)

    xbuf[SUBLANES:SUBLANES + tm, :] = xl_ref[...]
    xc = cb_ref[...]
    for jj in range(CONV_WIDTH):
        xc = xc + xbuf[pl.ds(SUBLANES - hist + jj, tm), :] * cw_ref[jj:jj + 1, :]
    a, u = _lru_gates(xc, wg_ref, bg_ref, lam_ref)
    _store_blocked(a_sc, slice(None), a)
    _store_blocked(u_sc, slice(None), u)

    def step(i, carry):
        p, hh = carry
        rows = pl.ds(i, SUBLANES, stride=seg)
        ai = _load_blocked(a_sc, rows)
        p = ai * p
        hh = ai * hh + _load_blocked(u_sc, rows)
        _store_blocked(a_sc, rows, p)
        _store_blocked(u_sc, rows, hh)
        return p, hh

    p_end, h_end = lax.fori_loop(0, seg, step,
                                 (jnp.ones((SUBLANES, w), F32), jnp.zeros((SUBLANES, w), F32)))

    h_in = hcar[...]
    for s in range(SUBLANES):
        blk = slice(s * seg, (s + 1) * seg)
        hs = _load_blocked(u_sc, blk) + _load_blocked(a_sc, blk) * h_in
        y_ref[blk, :] = (hs * _gelu_tanh(gl_ref[blk, :])).astype(y_ref.dtype)
        h_in = p_end[s:s + 1, :] * h_in + h_end[s:s + 1, :]
    hcar[...] = h_in
    xbuf[0:SUBLANES, :] = xbuf[tm:tm + SUBLANES, :]

    @pl.when(t == nt - 1)
    def _fin():
        hlast_ref[...] = h_in
        conv_ref[...] = xbuf[pl.ds(SUBLANES - hist, hist), :]


def _lru_prompt(xl, gl, conv_w, conv_b, wg_bf, bg, lru_lambda, b, s, tm):
    w = xl.shape[-1]
    nt = s // tm
    hist = CONV_WIDTH - 1
    full = lambda shape: pl.BlockSpec(shape, lambda i, j: (0,) * len(shape))
    return pl.pallas_call(
        _lru_prompt_kernel,
        grid=(b, nt),
        in_specs=[pl.BlockSpec((tm, w), lambda i, j: (i * nt + j, 0)),
                  pl.BlockSpec((tm, w), lambda i, j: (i * nt + j, 0)),
                  full((CONV_WIDTH, w)), full((1, w)), full(wg_bf.shape), full((1, 2 * w)),
                  full((1, w))],
        out_specs=[pl.BlockSpec((tm, w), lambda i, j: (i * nt + j, 0)),
                   pl.BlockSpec((None, hist, w), lambda i, j: (i, 0, 0)),
                   pl.BlockSpec((None, 1, w), lambda i, j: (i, 0, 0))],
        out_shape=[jax.ShapeDtypeStruct((b * s, w), BF16),
                   jax.ShapeDtypeStruct((b, hist, w), F32),
                   jax.ShapeDtypeStruct((b, 1, w), F32)],
        scratch_shapes=[pltpu.VMEM((tm + SUBLANES, w), F32), _blocked(tm, w), _blocked(tm, w),
                        pltpu.VMEM((1, w), F32)],
        compiler_params=_params("arbitrary", "arbitrary"),
        name="lru_prompt",
    )(xl, gl, conv_w, conv_b.reshape(1, w), wg_bf, bg.reshape(1, 2 * w), lru_lambda.reshape(1, w))


def _lru_sample_kernel(xl_ref, gl_ref, cbuf_ref, h0_ref, cw_ref, cb_ref, wg_ref, bg_ref, lam_ref,
                       y_ref, conv_ref, hlast_ref, x_sc, g_sc, c_sc, y_sc, *, t_new):
    nb = h0_ref.shape[0]
    hist = CONV_WIDTH - 1
    _store_blocked(x_sc, slice(None), xl_ref[...])
    _store_blocked(g_sc, slice(None), gl_ref[...])
    _store_blocked(c_sc, slice(None), cbuf_ref[...])
    xp = [_load_blocked(c_sc, pl.ds(jj, nb, stride=hist)) for jj in range(hist)]
    xp += [_load_blocked(x_sc, pl.ds(tt, nb, stride=t_new)) for tt in range(t_new)]
    hh = h0_ref[...]
    for tt in range(t_new):
        xc = cb_ref[...]
        for jj in range(CONV_WIDTH):
            xc = xc + xp[tt + jj] * cw_ref[jj:jj + 1, :]
        a, u = _lru_gates(xc, wg_ref, bg_ref, lam_ref)
        hh = a * hh + u
        gate = _gelu_tanh(_load_blocked(g_sc, pl.ds(tt, nb, stride=t_new)))
        _store_blocked(y_sc, pl.ds(tt, nb, stride=t_new), hh * gate)
    hlast_ref[...] = hh
    y_ref[...] = _load_blocked(y_sc, slice(None)).astype(y_ref.dtype)
    for jj in range(hist):
        _store_blocked(c_sc, pl.ds(jj, nb, stride=hist), xp[t_new + jj])
    conv_ref[...] = _load_blocked(c_sc, slice(None))


def _lru_sample(xl, gl, conv_buf, h0, conv_w, conv_b, wg_bf, bg, lru_lambda, t_new):
    n_tok, w = xl.shape
    nb = n_tok // t_new
    hist = CONV_WIDTH - 1
    full = lambda shape: pl.BlockSpec(shape, lambda i: (0,) * len(shape))
    return pl.pallas_call(
        functools.partial(_lru_sample_kernel, t_new=t_new),
        grid=(1,),
        in_specs=[full((n_tok, w)), full((n_tok, w)), full((nb * hist, w)), full((nb, w)),
                  full((CONV_WIDTH, w)), full((1, w)), full(wg_bf.shape), full((1, 2 * w)),
                  full((1, w))],
        out_specs=[full((n_tok, w)), full((nb * hist, w)), full((nb, w))],
        out_shape=[jax.ShapeDtypeStruct((n_tok, w), BF16),
                   jax.ShapeDtypeStruct((nb * hist, w), F32),
                   jax.ShapeDtypeStruct((nb, w), F32)],
        scratch_shapes=[_blocked(n_tok, w), _blocked(n_tok, w), _blocked(nb * hist, w),
                        _blocked(n_tok, w)],
        compiler_params=_params("arbitrary"),
        name="lru_sample",
    )(xl, gl, conv_buf.reshape(nb * hist, w), h0, conv_w, conv_b.reshape(1, w), wg_bf,
      bg.reshape(1, 2 * w), lru_lambda.reshape(1, w))


def _mlp_kernel(x_ref, o_ref, y_ref, gt1_ref, sc2_ref, sh2_ref, gt2_ref, g2_ref, gf_ref,
                wout_ref, w1_ref, w2_ref, out_ref, *, ff_chunk, final_norm):
    bb, tt, d = x_ref.shape
    rows = bb * tt
    mix_in = jnp.concatenate([o_ref[...], y_ref[...]], axis=1)
    mix = jnp.dot(mix_in, wout_ref[...], preferred_element_type=F32).reshape(bb, tt, d)
    x1 = x_ref[...] + gt1_ref[...] * mix
    h2 = (_rms(x1, g2_ref[...]) * (1.0 + sc2_ref[...]) + sh2_ref[...]).reshape(rows, d).astype(BF16)
    d_ff = w1_ref.shape[1]
    ff = jnp.zeros((rows, d), F32)
    for c in range(d_ff // ff_chunk):
        cs = slice(c * ff_chunk, (c + 1) * ff_chunk)
        hc = jnp.dot(h2, w1_ref[:, cs], preferred_element_type=F32)
        hc = jnp.square(jnp.maximum(hc, 0.0)).astype(BF16)
        ff = ff + jnp.dot(hc, w2_ref[cs, :], preferred_element_type=F32)
    x2 = x1 + gt2_ref[...] * ff.reshape(bb, tt, d)
    out_ref[...] = _rms(x2, gf_ref[...]) if final_norm else x2


def _mlp(x, o, y, gt1, sc2, sh2, gt2, g2, gf, wout_bf, w1_bf, w2_bf, bb, tt, final_norm):
    b, t, d = x.shape
    nt = t // tt
    rows = bb * tt
    mix_w = o.shape[-1]
    mod = pl.BlockSpec((bb, 1, d), lambda i, j: (i, 0, 0))
    gain = pl.BlockSpec((1, 1, d), lambda i, j: (0, 0, 0))
    wspec = lambda w: pl.BlockSpec(w.shape, lambda i, j: (0, 0), pipeline_mode=pl.Buffered(1))
    return pl.pallas_call(
        functools.partial(_mlp_kernel, ff_chunk=min(1024, w1_bf.shape[1]), final_norm=final_norm),
        grid=(b // bb, nt),
        in_specs=[pl.BlockSpec((bb, tt, d), lambda i, j: (i, j, 0)),
                  pl.BlockSpec((rows, mix_w), lambda i, j: (i * nt + j, 0)),
                  pl.BlockSpec((rows, y.shape[-1]), lambda i, j: (i * nt + j, 0)),
                  mod, mod, mod, mod, gain, gain, wspec(wout_bf), wspec(w1_bf), wspec(w2_bf)],
        out_specs=pl.BlockSpec((bb, tt, d), lambda i, j: (i, j, 0)),
        out_shape=jax.ShapeDtypeStruct((b, t, d), F32),
        compiler_params=_params("arbitrary", "arbitrary"),
        name="out_mlp",
    )(x, o, y, gt1, sc2, sh2, gt2, g2.reshape(1, 1, d), gf.reshape(1, 1, d), wout_bf, w1_bf, w2_bf)


def _block_diag(w):
    h, i, j = w.shape
    eye = jnp.eye(h, dtype=w.dtype)
    return (eye[:, None, :, None] * w[:, :, None, :]).reshape(h * i, h * j)


def _pick(n, target):
    t = min(n, target)
    while n % t:
        t -= 1
    return t


def kernel(x_prompt, x_sample, c_prompt, c_sample, cache_k, cache_v, page_table, state_h, state_conv, w_ada, b_ada, g_norm1, g_norm2, w_in, lambda_q1, lambda_k1, lambda_q2, lambda_k2, g_subln, conv_w, conv_b, w_rg, b_rg, w_ig, b_ig, lru_lambda, w_out, w_ff1, w_ff2, g_final):
    depth = w_in.shape[0]
    bp, seq, d = x_prompt.shape
    bs, t_new, _ = x_sample.shape
    lru_w = conv_w.shape[-1]
    da_w = (w_in.shape[-1] - 2 * lru_w) // 3
    dv = da_w // N_DA_HEADS
    n_pool, page_size = cache_k.shape[1], cache_k.shape[2]
    n_pages = page_table.shape[1]
    hist = CONV_WIDTH - 1

    slopes = 2.0 ** (-8.0 * jnp.arange(1, N_DA_HEADS + 1, dtype=F32) / N_DA_HEADS)
    cache_k2 = cache_k.reshape(depth * n_pool, page_size * N_DA_HEADS, dv)
    cache_v2 = cache_v.reshape(depth * n_pool, page_size * N_DA_HEADS, dv)

    tt_p = _pick(seq, 512)
    bb_s = _pick(bs, 64)
    tq = _pick(seq, 256)
    n_pg = _pick(n_pages, 16)
    tm_lru = _pick(seq, 512)

    yp, ys = x_prompt, x_sample
    kp_l, vp_l, cp_l, hp_l, ks_l, vs_l, cs_l, hs_l = [], [], [], [], [], [], [], []
    for l in range(depth):
        lam0 = _lambda_init(l)
        w_in_bf = w_in[l].astype(BF16)
        w_out_bf = w_out[l].astype(BF16)
        w1_bf = w_ff1[l].astype(BF16)
        w2_bf = w_ff2[l].astype(BF16)
        wg_bf = jnp.concatenate([_block_diag(w_rg[l]), _block_diag(w_ig[l])], axis=1).astype(BF16)
        bg = jnp.concatenate([b_rg[l], b_ig[l]])
        lamp = jnp.stack([lambda_q1[l], lambda_k1[l], lambda_q2[l], lambda_k2[l]])
        final = l == depth - 1

        mod = _ada_mod(jnp.concatenate([c_prompt, c_sample], axis=0), w_ada[l], b_ada[l])
        mods_p = [m[:, None, :] for m in jnp.split(mod[:bp], 6, axis=-1)]
        mods_s = [m[:, None, :] for m in jnp.split(mod[bp:], 6, axis=-1)]

        q, k, v, kb, vb, xl, gl = _in_proj(yp, mods_p[1], mods_p[0], g_norm1[l], w_in_bf,
                                           1, tt_p, da_w, lru_w)
        o = _prompt_attention(q.reshape(bp, seq, da_w), kb.reshape(bp, seq, da_w),
                              vb.reshape(bp, seq, da_w), lamp, g_subln[l], slopes, lam0, tq)
        y_lru, cp, hp = _lru_prompt(xl, gl, conv_w[l], conv_b[l], wg_bf, bg, lru_lambda[l],
                                    bp, seq, tm_lru)
        yp = _mlp(yp, o.reshape(bp * seq, da_w), y_lru, mods_p[2], mods_p[4], mods_p[3], mods_p[5],
                  g_norm2[l], g_final, w_out_bf, w1_bf, w2_bf, 1, tt_p, final)
        kp_l.append(k.reshape(bp, seq, N_DA_HEADS, dv))
        vp_l.append(v.reshape(bp, seq, N_DA_HEADS, dv))
        cp_l.append(cp)
        hp_l.append(hp.reshape(bp, lru_w))

        q, k, v, _, _, xl, gl = _in_proj(ys, mods_s[1], mods_s[0], g_norm1[l], w_in_bf,
                                         bb_s, t_new, da_w, lru_w)
        o = _sample_attention(q.reshape(bs, t_new, da_w),
                              k.reshape(bs, t_new * N_DA_HEADS, dv),
                              v.reshape(bs, t_new * N_DA_HEADS, dv),
                              cache_k2, cache_v2, page_table, l * n_pool, lamp, g_subln[l],
                              lam0, n_pg)
        y_lru, cs, hs = _lru_sample(xl, gl, state_conv[l], state_h[l], conv_w[l], conv_b[l],
                                    wg_bf, bg, lru_lambda[l], t_new)
        ys = _mlp(ys, o.reshape(bs * t_new, da_w), y_lru, mods_s[2], mods_s[4], mods_s[3],
                  mods_s[5], g_norm2[l], g_final, w_out_bf, w1_bf, w2_bf, bb_s, t_new, final)
        ks_l.append(k.reshape(bs, t_new, N_DA_HEADS, dv))
        vs_l.append(v.reshape(bs, t_new, N_DA_HEADS, dv))
        cs_l.append(cs.reshape(bs, hist, lru_w))
        hs_l.append(hs)

    return (yp, ys,
            jnp.stack(kp_l), jnp.stack(vp_l), jnp.stack(cp_l), jnp.stack(hp_l),
            jnp.stack(ks_l), jnp.stack(vs_l), jnp.stack(cs_l), jnp.stack(hs_l))
```

```python
import functools
import math

import jax
import jax.numpy as jnp
from jax import lax
from jax.experimental import pallas as pl
from jax.experimental.pallas import tpu as pltpu

F32 = jnp.float32
BF16 = jnp.bfloat16

N_DA_HEADS = 4
N_LRU_HEADS = 8
CONV_WIDTH = 4
LRU_C = 8.0
EPS = 1e-6
NEG_BIG = -1e30
SUBLANES = 8
LANES = 128
VMEM_LIMIT_BYTES = 56 * 1024 * 1024
SAMPLE_RING_SLOTS = 3
LRU_SEGMENTS = SUBLANES
LRU_SEG_PAD = SUBLANES


def _lambda_init(layer):
    return 0.8 - 0.6 * math.exp(-0.3 * layer)


def _params(*sem):
    return pltpu.CompilerParams(dimension_semantics=sem, vmem_limit_bytes=VMEM_LIMIT_BYTES)


def _rms(x, g):
    return x * lax.rsqrt(jnp.mean(x * x, axis=-1, keepdims=True) + EPS) * g


def _ada_kernel(c_ref, w_ref, b_ref, o_ref):
    c = c_ref[...]
    s = (c * jax.nn.sigmoid(c)).astype(BF16)
    o_ref[...] = jnp.dot(s, w_ref[...].astype(BF16), preferred_element_type=F32) + b_ref[...]


def _ada_mod(c, w, b, tn=1536):
    m, d = c.shape
    n = w.shape[1]
    return pl.pallas_call(
        _ada_kernel,
        grid=(n // tn,),
        in_specs=[pl.BlockSpec((m, d), lambda j: (0, 0)),
                  pl.BlockSpec((d, tn), lambda j: (0, j)),
                  pl.BlockSpec((1, tn), lambda j: (0, j))],
        out_specs=pl.BlockSpec((m, tn), lambda j: (0, j)),
        out_shape=jax.ShapeDtypeStruct((m, n), F32),
        compiler_params=_params("arbitrary"),
        name="ada_mod",
    )(c, w, b.reshape(1, n))


def _inproj_kernel(x_ref, sc_ref, sh_ref, g_ref, w_ref, *out_refs, da_w, lru_w, q_scale, tk):
    bb, tt, d = x_ref.shape
    rows = bb * tt
    dv = da_w // N_DA_HEADS
    h = _rms(x_ref[...], g_ref[...]) * (1.0 + sc_ref[...]) + sh_ref[...]
    h = h.reshape(rows, d).astype(BF16)
    proj = jnp.dot(h, w_ref[...], preferred_element_type=F32)
    q = proj[:, :da_w] * q_scale
    k = proj[:, da_w:2 * da_w]
    v = proj[:, 2 * da_w:3 * da_w]
    if tk is None:
        q_ref, k4_ref, v4_ref, xl_ref, gl_ref = out_refs
        q_ref[...] = q.astype(BF16)
    else:
        qt_ref, k4_ref, v4_ref, kb_ref, vt_ref, xl_ref, gl_ref = out_refs
        qt_ref[...] = q.T.astype(BF16)
        kb_ref[...] = k.astype(BF16)
        vt = v.T.astype(BF16)
        for c in range(rows // tk):
            vt_ref[c] = vt[:, c * tk:(c + 1) * tk]
    for hh in range(N_DA_HEADS):
        k4_ref[pl.ds(hh, rows, stride=N_DA_HEADS), :] = k[:, hh * dv:(hh + 1) * dv]
        v4_ref[pl.ds(hh, rows, stride=N_DA_HEADS), :] = v[:, hh * dv:(hh + 1) * dv]
    xl_ref[...] = proj[:, 3 * da_w:3 * da_w + lru_w]
    gl_ref[...] = proj[:, 3 * da_w + lru_w:]


def _in_proj(x, sc, sh, g, w_bf, bb, tt, da_w, lru_w, tk=None):
    b, t, d = x.shape
    n_tok = b * t
    nt = t // tt
    rows = bb * tt
    dv = da_w // N_DA_HEADS
    dqk = dv // 2

    def flat(width, dtype, mult=1):
        return (pl.BlockSpec((rows * mult, width), lambda i, j: (i * nt + j, 0)),
                jax.ShapeDtypeStruct((n_tok * mult, width), dtype))

    kv4 = [flat(dv, F32, N_DA_HEADS), flat(dv, F32, N_DA_HEADS)]
    lru = [flat(lru_w, F32), flat(lru_w, F32)]
    if tk is None:
        outs = [flat(da_w, BF16)] + kv4 + lru
    else:
        assert bb == 1 and tt % tk == 0
        qt = (pl.BlockSpec((None, da_w, tt), lambda i, j: (i, 0, j)),
              jax.ShapeDtypeStruct((b, da_w, t), BF16))
        vt = (pl.BlockSpec((None, tt // tk, da_w, tk), lambda i, j: (i, j, 0, 0)),
              jax.ShapeDtypeStruct((b, t // tk, da_w, tk), BF16))
        outs = [qt] + kv4 + [flat(da_w, BF16), vt] + lru
    return pl.pallas_call(
        functools.partial(_inproj_kernel, da_w=da_w, lru_w=lru_w, q_scale=dqk ** -0.5, tk=tk),
        grid=(b // bb, nt),
        in_specs=[pl.BlockSpec((bb, tt, d), lambda i, j: (i, j, 0)),
                  pl.BlockSpec((bb, 1, d), lambda i, j: (i, 0, 0)),
                  pl.BlockSpec((bb, 1, d), lambda i, j: (i, 0, 0)),
                  pl.BlockSpec((1, 1, d), lambda i, j: (0, 0, 0)),
                  pl.BlockSpec(w_bf.shape, lambda i, j: (0, 0))],
        out_specs=[o[0] for o in outs],
        out_shape=[o[1] for o in outs],
        compiler_params=_params("arbitrary", "arbitrary"),
        name="in_proj",
    )(x, sc, sh, g.reshape(1, 1, d), w_bf)


def _lam_value(lamp_ref, lam0):
    lp = lamp_ref[...]
    t1 = jnp.sum(lp[0:1] * lp[1:2], axis=-1, keepdims=True)
    t2 = jnp.sum(lp[2:3] * lp[3:4], axis=-1, keepdims=True)
    return jnp.exp(t1) - jnp.exp(t2) + lam0


def _softmax_update(s, m_sc, l_sc, acc_sc, pv_fn):
    m_old = m_sc[...]
    m_new = jnp.maximum(m_old, jnp.max(s, axis=-1, keepdims=True))
    alpha = jnp.exp(m_old - m_new)
    p = jnp.exp(s - m_new)
    l_sc[...] = alpha * l_sc[...] + jnp.sum(p, axis=-1, keepdims=True)
    acc_sc[...] = alpha * acc_sc[...] + pv_fn(p.astype(BF16))
    m_sc[...] = m_new


def _nt_dot(a, b):
    return lax.dot_general(a, b, (((1,), (1,)), ((), ())), preferred_element_type=F32)


def _pattn_kernel(slopes_ref, qt_ref, k_ref, vt_ref, lamp_ref, g_ref, o_ref, m_sc, l_sc, acc_sc,
                  s_sc, *, tq, lam0):
    qi = pl.program_id(1)
    dv = g_ref.shape[0]
    heads = range(N_DA_HEADS)

    row = lax.broadcasted_iota(jnp.int32, (dv, tq), 0)
    zero = jnp.zeros((dv, tq), BF16)
    qst = []
    for h in heads:
        qt = qt_ref[h * dv:(h + 1) * dv, :]
        qst.append(jnp.concatenate([jnp.where(row < dv // 2, qt, zero),
                                    jnp.where(row >= dv // 2, qt, zero)], axis=1))

    m_sc[...] = jnp.full(m_sc.shape, NEG_BIG, F32)
    l_sc[...] = jnp.zeros(l_sc.shape, F32)
    acc_sc[...] = jnp.zeros(acc_sc.shape, F32)

    key = lax.broadcasted_iota(jnp.int32, (tq, LANES), 0).astype(F32)
    bias = [jnp.concatenate([slopes_ref[h] * key] * (2 * tq // LANES), axis=1) for h in heads]

    cols = [slice(h * dv, (h + 1) * dv) for h in heads]

    def scores(j, slot):
        start = pl.multiple_of(j * tq, tq)
        for h in heads:
            s_sc[slot, h] = jnp.dot(k_ref[pl.ds(start, tq), cols[h]], qst[h],
                                    preferred_element_type=F32)

    def update(j, slot, masked):
        if masked:
            kk = lax.broadcasted_iota(jnp.int32, (tq, 2 * tq), 0)
            qq = lax.broadcasted_iota(jnp.int32, (tq, 2 * tq), 1)
            visible = kk <= jnp.where(qq >= tq, qq - tq, qq)
        tile_off = ((j - qi) * tq).astype(F32)
        for h in heads:
            s = s_sc[slot, h] + bias[h]
            if masked:
                s = jnp.where(visible, s, NEG_BIG)
            shift = slopes_ref[h] * tile_off
            m_old = m_sc[h]
            m_new = jnp.maximum(m_old, jnp.max(s, axis=0, keepdims=True) + shift)
            alpha = jnp.exp(m_old - m_new)
            p = jnp.exp(s - (m_new - shift))
            l_sc[h] = alpha * l_sc[h] + jnp.sum(p, axis=0, keepdims=True)
            acc_sc[h] = alpha * acc_sc[h] + jnp.dot(vt_ref[j, cols[h], :], p.astype(BF16),
                                                    preferred_element_type=F32)
            m_sc[h] = m_new

    def pair(i, carry):
        j = 2 * i
        scores(j + 1, 1)
        update(j, 0, False)
        scores(j + 2, 0)
        update(j + 1, 1, False)
        return carry

    scores(0, 0)
    lax.fori_loop(0, qi // 2, pair, 0)

    @pl.when(qi % 2 == 0)
    def _diag_even():
        update(qi, 0, True)

    @pl.when(qi % 2 == 1)
    def _diag_odd():
        scores(qi, 1)
        update(qi - 1, 0, False)
        update(qi, 1, True)

    lam = _lam_value(lamp_ref, lam0)
    outs = []
    for h in heads:
        on = acc_sc[h] * (1.0 / l_sc[h])
        ot = on[:, :tq] - lam * on[:, tq:]
        ms = jnp.mean(ot * ot, axis=0, keepdims=True)
        outs.append((ot * lax.rsqrt(ms + EPS) * g_ref[...] * (1.0 - lam0)).T)
    o_ref[...] = jnp.concatenate(outs, axis=1).astype(o_ref.dtype)


def _prompt_attention(qt, kb, vt, lamp, g_subln, slopes, lam0, tq):
    b, s, da_w = kb.shape
    dv = da_w // N_DA_HEADS
    kernel = functools.partial(_pattn_kernel, tq=tq, lam0=lam0)
    return pl.pallas_call(
        kernel,
        grid_spec=pltpu.PrefetchScalarGridSpec(
            num_scalar_prefetch=1,
            grid=(b, s // tq),
            in_specs=[pl.BlockSpec((None, da_w, tq), lambda i, j, sl: (i, 0, j)),
                      pl.BlockSpec((None, s, da_w), lambda i, j, sl: (i, 0, 0)),
                      pl.BlockSpec((None, s // tq, da_w, tq), lambda i, j, sl: (i, 0, 0, 0)),
                      pl.BlockSpec(lamp.shape, lambda i, j, sl: (0, 0)),
                      pl.BlockSpec((dv, 1), lambda i, j, sl: (0, 0))],
            out_specs=pl.BlockSpec((None, tq, da_w), lambda i, j, sl: (i, j, 0)),
            scratch_shapes=[pltpu.VMEM((N_DA_HEADS, 1, 2 * tq), F32),
                            pltpu.VMEM((N_DA_HEADS, 1, 2 * tq), F32),
                            pltpu.VMEM((N_DA_HEADS, dv, 2 * tq), F32),
                            pltpu.VMEM((2, N_DA_HEADS, tq, 2 * tq), F32)]),
        out_shape=jax.ShapeDtypeStruct((b, s, da_w), BF16),
        compiler_params=_params("arbitrary", "arbitrary"),
        name="prompt_attn",
    )(slopes, qt, kb, vt, lamp, g_subln.reshape(dv, 1))


def _sattn_kernel(pt_ref, q_ref, kn_ref, vn_ref, lamp_ref, g_ref, kc_hbm, vc_hbm, o_ref,
                  kbuf, vbuf, sem, m_sc, l_sc, acc_sc, *, n_seq, nj, n_pg, pg_rows, past, lam0,
                  pool_off):
    seq = pl.program_id(0)
    j = pl.program_id(1)
    n_chunks = n_seq * nj
    n_slots = kbuf.shape[0]
    g = seq * nj + j
    t_new, da_w = q_ref.shape
    dv = da_w // N_DA_HEADS
    rows_h = 2 * t_new
    n_rows = N_DA_HEADS * rows_h
    n_keys = n_pg * pg_rows // N_DA_HEADS

    def chunk_copies(chunk, slot):
        cs = chunk // nj
        cj = chunk % nj
        copies = []
        for i in range(n_pg):
            page = pt_ref[cs, cj * n_pg + i] + pool_off
            rows = pl.ds(i * pg_rows, pg_rows)
            copies.append(pltpu.make_async_copy(kc_hbm.at[page], kbuf.at[slot, rows], sem.at[0, slot]))
            copies.append(pltpu.make_async_copy(vc_hbm.at[page], vbuf.at[slot, rows], sem.at[1, slot]))
        return copies

    @pl.when(g == 0)
    def _prologue():
        for c in range(min(n_slots - 1, n_chunks)):
            for cp in chunk_copies(c, c):
                cp.start()

    ahead = g + (n_slots - 1)

    @pl.when(ahead < n_chunks)
    def _prefetch():
        for cp in chunk_copies(ahead, ahead % n_slots):
            cp.start()

    slot = g % n_slots
    for cp in chunk_copies(g, slot):
        cp.wait()

    row1 = lax.broadcasted_iota(jnp.int32, (n_rows, 1), 0)
    h_row = row1 // rows_h
    q_row = row1 % t_new
    slope_row = jnp.exp2(-8.0 * (h_row + 1).astype(F32) / N_DA_HEADS)

    qf = q_ref[...].astype(F32)
    lane = lax.broadcasted_iota(jnp.int32, (t_new, dv), 1)
    pieces = []
    for h in range(N_DA_HEADS):
        qh = qf[:, h * dv:(h + 1) * dv]
        pieces.append(jnp.where(lane < dv // 2, qh, 0.0))
        pieces.append(jnp.where(lane >= dv // 2, qh, 0.0))
    qall = jnp.concatenate(pieces, axis=0).astype(BF16)

    @pl.when(j == 0)
    def _init():
        m_sc[...] = jnp.full(m_sc.shape, NEG_BIG, F32)
        l_sc[...] = jnp.zeros(l_sc.shape, F32)
        acc_sc[...] = jnp.zeros(acc_sc.shape, F32)

    def head_rows(buf, h):
        return buf[slot, pl.ds(h, n_keys, stride=N_DA_HEADS), :].astype(BF16)

    s = jnp.concatenate([_nt_dot(qall[h * rows_h:(h + 1) * rows_h], head_rows(kbuf, h))
                         for h in range(N_DA_HEADS)], axis=0)
    kpos = lax.broadcasted_iota(jnp.int32, (1, n_keys), 1) + (j * n_keys - past)
    s = s + slope_row * kpos.astype(F32)

    def pv(p):
        return jnp.concatenate(
            [jnp.dot(p[h * rows_h:(h + 1) * rows_h], head_rows(vbuf, h), preferred_element_type=F32)
             for h in range(N_DA_HEADS)], axis=0)

    _softmax_update(s, m_sc, l_sc, acc_sc, pv)

    @pl.when(j == nj - 1)
    def _finish():
        n_new = kn_ref.shape[0]
        pad = jnp.zeros((LANES - n_new, dv), F32)
        kn = jnp.concatenate([kn_ref[...], pad], axis=0).astype(BF16)
        vn = jnp.concatenate([vn_ref[...], pad], axis=0).astype(BF16)
        c = lax.broadcasted_iota(jnp.int32, (n_rows, LANES), 1)
        key = c // N_DA_HEADS
        ok = (c % N_DA_HEADS == h_row) & (key <= q_row) & (c < n_new)
        sn = jnp.where(ok, _nt_dot(qall, kn) + slope_row * key.astype(F32), NEG_BIG)
        _softmax_update(sn, m_sc, l_sc, acc_sc,
                        lambda p: jnp.dot(p, vn, preferred_element_type=F32))

        lam = _lam_value(lamp_ref, lam0)
        on = acc_sc[...] / l_sc[...]
        outs = []
        for h in range(N_DA_HEADS):
            o1 = on[h * rows_h:h * rows_h + t_new]
            o2 = on[h * rows_h + t_new:(h + 1) * rows_h]
            outs.append(_rms(o1 - lam * o2, g_ref[...]) * (1.0 - lam0))
        o_ref[...] = jnp.concatenate(outs, axis=1).astype(o_ref.dtype)


def _sample_attention(q, k_new, v_new, cache_k2, cache_v2, page_table, pool_off, lamp, g_subln,
                      lam0, n_pg):
    b, t_new, da_w = q.shape
    dv = da_w // N_DA_HEADS
    n_pages = page_table.shape[1]
    pg_rows = cache_k2.shape[1]
    past = n_pages * (pg_rows // N_DA_HEADS)
    n_rows = N_DA_HEADS * 2 * t_new
    nj = n_pages // n_pg
    kernel = functools.partial(_sattn_kernel, n_seq=b, nj=nj, n_pg=n_pg, pg_rows=pg_rows, past=past,
                               lam0=lam0, pool_off=pool_off)
    in_specs = [pl.BlockSpec((None, t_new, da_w), lambda s, j, pt: (s, 0, 0)),
                pl.BlockSpec((None, t_new * N_DA_HEADS, dv), lambda s, j, pt: (s, 0, 0)),
                pl.BlockSpec((None, t_new * N_DA_HEADS, dv), lambda s, j, pt: (s, 0, 0)),
                pl.BlockSpec(lamp.shape, lambda s, j, pt: (0, 0)),
                pl.BlockSpec((1, dv), lambda s, j, pt: (0, 0)),
                pl.BlockSpec(memory_space=pl.ANY),
                pl.BlockSpec(memory_space=pl.ANY)]
    ring = pltpu.VMEM((SAMPLE_RING_SLOTS, n_pg * pg_rows, dv), F32)
    return pl.pallas_call(
        kernel,
        grid_spec=pltpu.PrefetchScalarGridSpec(
            num_scalar_prefetch=1,
            grid=(b, nj),
            in_specs=in_specs,
            out_specs=pl.BlockSpec((None, t_new, da_w), lambda s, j, pt: (s, 0, 0)),
            scratch_shapes=[ring, ring, pltpu.SemaphoreType.DMA((2, SAMPLE_RING_SLOTS)),
                            pltpu.VMEM((n_rows, 1), F32), pltpu.VMEM((n_rows, 1), F32),
                            pltpu.VMEM((n_rows, dv), F32)]),
        out_shape=jax.ShapeDtypeStruct((b, t_new, da_w), BF16),
        compiler_params=_params("arbitrary", "arbitrary"),
        name="sample_attn",
    )(page_table, q, k_new, v_new, lamp, g_subln.reshape(1, dv), cache_k2, cache_v2)


def _softplus(z):
    return jnp.maximum(z, 0.0) + jnp.log1p(jnp.exp(-jnp.abs(z)))


def _gelu_tanh(x):
    return 0.5 * x * (1.0 + jnp.tanh(math.sqrt(2.0 / math.pi) * (x + 0.044715 * (x * x * x))))


def _lru_gates(xc, wg_ref, bg_ref, lam_ref):
    w = xc.shape[-1]
    g = jnp.dot(xc.astype(BF16), wg_ref[...], preferred_element_type=F32) + bg_ref[...]
    r = jax.nn.sigmoid(g[:, :w])
    ig = jax.nn.sigmoid(g[:, w:])
    log_a = -LRU_C * r * _softplus(-lam_ref[...])
    a = jnp.exp(log_a)
    u = jnp.sqrt(-jnp.tanh(log_a) * (1.0 + a * a)) * (ig * xc)
    return a, u


def _load_blocked(sc, rows):
    return jnp.concatenate([sc[c, rows, :] for c in range(sc.shape[0])], axis=1)


def _store_blocked(sc, rows, val):
    for c in range(sc.shape[0]):
        sc[c, rows, :] = val[:, c * LANES:(c + 1) * LANES]


def _blocked(rows, w):
    return pltpu.VMEM((w // LANES, rows, LANES), F32)


def _lru_prompt_kernel(xl_ref, gl_ref, cw_ref, cb_ref, wg_ref, bg_ref, lam_ref,
                       y_ref, conv_ref, hlast_ref, xbuf, a_sc, u_sc, hcar):
    t = pl.program_id(1)
    nt = pl.num_programs(1)
    tm, w = xl_ref.shape
    seg = tm // LRU_SEGMENTS
    pitch = seg + LRU_SEG_PAD
    hist = CONV_WIDTH - 1

    @pl.when(t == 0)
    def _init():
        xbuf[0:SUBLANES, :] = jnp.zeros((SUBLANES, w), F32)
        hcar[...] = jnp.zeros(hcar.shape, F32)

    xbuf[SUBLANES:SUBLANES + tm, :] = xl_ref[...]
    xc = cb_ref[...]
    for jj in range(CONV_WIDTH):
        xc = xc + xbuf[pl.ds(SUBLANES - hist + jj, tm), :] * cw_ref[jj:jj + 1, :]
    a, u = _lru_gates(xc, wg_ref, bg_ref, lam_ref)
    for s in range(LRU_SEGMENTS):
        _store_blocked(a_sc, slice(s * pitch, s * pitch + seg), a[s * seg:(s + 1) * seg])
        _store_blocked(u_sc, slice(s * pitch, s * pitch + seg), u[s * seg:(s + 1) * seg])

    def step(i, carry):
        p, hh = carry
        rows = pl.ds(i, LRU_SEGMENTS, stride=pitch)
        ai = _load_blocked(a_sc, rows)
        p = ai * p
        hh = ai * hh + _load_blocked(u_sc, rows)
        _store_blocked(a_sc, rows, p)
        _store_blocked(u_sc, rows, hh)
        return p, hh

    p_end, h_end = lax.fori_loop(0, seg, step,
                                 (jnp.ones((LRU_SEGMENTS, w), F32), jnp.zeros((LRU_SEGMENTS, w), F32)))

    h_in = hcar[...]
    for s in range(LRU_SEGMENTS):
        blk = slice(s * seg, (s + 1) * seg)
        sblk = slice(s * pitch, s * pitch + seg)
        hs = _load_blocked(u_sc, sblk) + _load_blocked(a_sc, sblk) * h_in
        y_ref[blk, :] = (hs * _gelu_tanh(gl_ref[blk, :])).astype(y_ref.dtype)
        h_in = p_end[s:s + 1, :] * h_in + h_end[s:s + 1, :]
    hcar[...] = h_in
    xbuf[0:SUBLANES, :] = xbuf[tm:tm + SUBLANES, :]

    @pl.when(t == nt - 1)
    def _fin():
        hlast_ref[...] = h_in
        conv_ref[...] = xbuf[pl.ds(SUBLANES - hist, hist), :]


def _lru_prompt(xl, gl, conv_w, conv_b, wg_bf, bg, lru_lambda, b, s, tm):
    w = xl.shape[-1]
    nt = s // tm
    hist = CONV_WIDTH - 1
    scan_rows = LRU_SEGMENTS * (tm // LRU_SEGMENTS + LRU_SEG_PAD)
    full = lambda shape: pl.BlockSpec(shape, lambda i, j: (0,) * len(shape))
    return pl.pallas_call(
        _lru_prompt_kernel,
        grid=(b, nt),
        in_specs=[pl.BlockSpec((tm, w), lambda i, j: (i * nt + j, 0)),
                  pl.BlockSpec((tm, w), lambda i, j: (i * nt + j, 0)),
                  full((CONV_WIDTH, w)), full((1, w)), full(wg_bf.shape), full((1, 2 * w)),
                  full((1, w))],
        out_specs=[pl.BlockSpec((tm, w), lambda i, j: (i * nt + j, 0)),
                   pl.BlockSpec((None, hist, w), lambda i, j: (i, 0, 0)),
                   pl.BlockSpec((None, 1, w), lambda i, j: (i, 0, 0))],
        out_shape=[jax.ShapeDtypeStruct((b * s, w), BF16),
                   jax.ShapeDtypeStruct((b, hist, w), F32),
                   jax.ShapeDtypeStruct((b, 1, w), F32)],
        scratch_shapes=[pltpu.VMEM((tm + SUBLANES, w), F32), _blocked(scan_rows, w),
                        _blocked(scan_rows, w), pltpu.VMEM((1, w), F32)],
        compiler_params=_params("arbitrary", "arbitrary"),
        name="lru_prompt",
    )(xl, gl, conv_w, conv_b.reshape(1, w), wg_bf, bg.reshape(1, 2 * w), lru_lambda.reshape(1, w))


def _lru_sample_kernel(xl_ref, gl_ref, cbuf_ref, h0_ref, cw_ref, cb_ref, wg_ref, bg_ref, lam_ref,
                       y_ref, conv_ref, hlast_ref, x_sc, g_sc, c_sc, y_sc, *, t_new):
    nb = h0_ref.shape[0]
    hist = CONV_WIDTH - 1
    _store_blocked(x_sc, slice(None), xl_ref[...])
    _store_blocked(g_sc, slice(None), gl_ref[...])
    _store_blocked(c_sc, slice(None), cbuf_ref[...])
    xp = [_load_blocked(c_sc, pl.ds(jj, nb, stride=hist)) for jj in range(hist)]
    xp += [_load_blocked(x_sc, pl.ds(tt, nb, stride=t_new)) for tt in range(t_new)]
    hh = h0_ref[...]
    for tt in range(t_new):
        xc = cb_ref[...]
        for jj in range(CONV_WIDTH):
            xc = xc + xp[tt + jj] * cw_ref[jj:jj + 1, :]
        a, u = _lru_gates(xc, wg_ref, bg_ref, lam_ref)
        hh = a * hh + u
        gate = _gelu_tanh(_load_blocked(g_sc, pl.ds(tt, nb, stride=t_new)))
        _store_blocked(y_sc, pl.ds(tt, nb, stride=t_new), hh * gate)
    hlast_ref[...] = hh
    y_ref[...] = _load_blocked(y_sc, slice(None)).astype(y_ref.dtype)
    for jj in range(hist):
        _store_blocked(c_sc, pl.ds(jj, nb, stride=hist), xp[t_new + jj])
    conv_ref[...] = _load_blocked(c_sc, slice(None))


def _lru_sample(xl, gl, conv_buf, h0, conv_w, conv_b, wg_bf, bg, lru_lambda, t_new):
    n_tok, w = xl.shape
    nb = n_tok // t_new
    hist = CONV_WIDTH - 1
    full = lambda shape: pl.BlockSpec(shape, lambda i: (0,) * len(shape))
    return pl.pallas_call(
        functools.partial(_lru_sample_kernel, t_new=t_new),
        grid=(1,),
        in_specs=[full((n_tok, w)), full((n_tok, w)), full((nb * hist, w)), full((nb, w)),
                  full((CONV_WIDTH, w)), full((1, w)), full(wg_bf.shape), full((1, 2 * w)),
                  full((1, w))],
        out_specs=[full((n_tok, w)), full((nb * hist, w)), full((nb, w))],
        out_shape=[jax.ShapeDtypeStruct((n_tok, w), BF16),
                   jax.ShapeDtypeStruct((nb * hist, w), F32),
                   jax.ShapeDtypeStruct((nb, w), F32)],
        scratch_shapes=[_blocked(n_tok, w), _blocked(n_tok, w), _blocked(nb * hist, w),
                        _blocked(n_tok, w)],
        compiler_params=_params("arbitrary"),
        name="lru_sample",
    )(xl, gl, conv_buf.reshape(nb * hist, w), h0, conv_w, conv_b.reshape(1, w), wg_bf,
      bg.reshape(1, 2 * w), lru_lambda.reshape(1, w))


def _mlp_kernel(x_ref, o_ref, y_ref, gt1_ref, sc2_ref, sh2_ref, gt2_ref, g2_ref, gf_ref,
                wout_ref, w1_ref, w2_ref, out_ref, *, ff_chunk, final_norm):
    bb, tt, d = x_ref.shape
    rows = bb * tt
    mix_in = jnp.concatenate([o_ref[...], y_ref[...]], axis=1)
    mix = jnp.dot(mix_in, wout_ref[...], preferred_element_type=F32).reshape(bb, tt, d)
    x1 = x_ref[...] + gt1_ref[...] * mix
    h2 = (_rms(x1, g2_ref[...]) * (1.0 + sc2_ref[...]) + sh2_ref[...]).reshape(rows, d).astype(BF16)
    d_ff = w1_ref.shape[1]
    ff = jnp.zeros((rows, d), F32)
    for c in range(d_ff // ff_chunk):
        cs = slice(c * ff_chunk, (c + 1) * ff_chunk)
        hc = jnp.dot(h2, w1_ref[:, cs], preferred_element_type=F32)
        hc = jnp.square(jnp.maximum(hc, 0.0)).astype(BF16)
        ff = ff + jnp.dot(hc, w2_ref[cs, :], preferred_element_type=F32)
    x2 = x1 + gt2_ref[...] * ff.reshape(bb, tt, d)
    out_ref[...] = _rms(x2, gf_ref[...]) if final_norm else x2


def _mlp(x, o, y, gt1, sc2, sh2, gt2, g2, gf, wout_bf, w1_bf, w2_bf, bb, tt, final_norm):
    b, t, d = x.shape
    nt = t // tt
    rows = bb * tt
    mix_w = o.shape[-1]
    mod = pl.BlockSpec((bb, 1, d), lambda i, j: (i, 0, 0))
    gain = pl.BlockSpec((1, 1, d), lambda i, j: (0, 0, 0))
    wspec = lambda w: pl.BlockSpec(w.shape, lambda i, j: (0, 0), pipeline_mode=pl.Buffered(1))
    return pl.pallas_call(
        functools.partial(_mlp_kernel, ff_chunk=min(1024, w1_bf.shape[1]), final_norm=final_norm),
        grid=(b // bb, nt),
        in_specs=[pl.BlockSpec((bb, tt, d), lambda i, j: (i, j, 0)),
                  pl.BlockSpec((rows, mix_w), lambda i, j: (i * nt + j, 0)),
                  pl.BlockSpec((rows, y.shape[-1]), lambda i, j: (i * nt + j, 0)),
                  mod, mod, mod, mod, gain, gain, wspec(wout_bf), wspec(w1_bf), wspec(w2_bf)],
        out_specs=pl.BlockSpec((bb, tt, d), lambda i, j: (i, j, 0)),
        out_shape=jax.ShapeDtypeStruct((b, t, d), F32),
        compiler_params=_params("arbitrary", "arbitrary"),
        name="out_mlp",
    )(x, o, y, gt1, sc2, sh2, gt2, g2.reshape(1, 1, d), gf.reshape(1, 1, d), wout_bf, w1_bf, w2_bf)


def _block_diag(w):
    h, i, j = w.shape
    eye = jnp.eye(h, dtype=w.dtype)
    return (eye[:, None, :, None] * w[:, :, None, :]).reshape(h * i, h * j)


def _pick(n, target):
    t = min(n, target)
    while n % t:
        t -= 1
    return t


def kernel(x_prompt, x_sample, c_prompt, c_sample, cache_k, cache_v, page_table, state_h, state_conv, w_ada, b_ada, g_norm1, g_norm2, w_in, lambda_q1, lambda_k1, lambda_q2, lambda_k2, g_subln, conv_w, conv_b, w_rg, b_rg, w_ig, b_ig, lru_lambda, w_out, w_ff1, w_ff2, g_final):
    depth = w_in.shape[0]
    bp, seq, d = x_prompt.shape
    bs, t_new, _ = x_sample.shape
    lru_w = conv_w.shape[-1]
    da_w = (w_in.shape[-1] - 2 * lru_w) // 3
    dv = da_w // N_DA_HEADS
    n_pool, page_size = cache_k.shape[1], cache_k.shape[2]
    n_pages = page_table.shape[1]
    hist = CONV_WIDTH - 1

    slopes = 2.0 ** (-8.0 * jnp.arange(1, N_DA_HEADS + 1, dtype=F32) / N_DA_HEADS)
    cache_k2 = cache_k.reshape(depth * n_pool, page_size * N_DA_HEADS, dv)
    cache_v2 = cache_v.reshape(depth * n_pool, page_size * N_DA_HEADS, dv)

    tt_p = _pick(seq, 512)
    bb_s = _pick(bs, 64)
    tq = _pick(seq, 256)
    n_pg = _pick(n_pages, 16)
    tm_lru = _pick(seq, 512)

    yp, ys = x_prompt, x_sample
    kp_l, vp_l, cp_l, hp_l, ks_l, vs_l, cs_l, hs_l = [], [], [], [], [], [], [], []
    for l in range(depth):
        lam0 = _lambda_init(l)
        w_in_bf = w_in[l].astype(BF16)
        w_out_bf = w_out[l].astype(BF16)
        w1_bf = w_ff1[l].astype(BF16)
        w2_bf = w_ff2[l].astype(BF16)
        wg_bf = jnp.concatenate([_block_diag(w_rg[l]), _block_diag(w_ig[l])], axis=1).astype(BF16)
        bg = jnp.concatenate([b_rg[l], b_ig[l]])
        lamp = jnp.stack([lambda_q1[l], lambda_k1[l], lambda_q2[l], lambda_k2[l]])
        final = l == depth - 1

        mod = _ada_mod(jnp.concatenate([c_prompt, c_sample], axis=0), w_ada[l], b_ada[l])
        mods_p = [m[:, None, :] for m in jnp.split(mod[:bp], 6, axis=-1)]
        mods_s = [m[:, None, :] for m in jnp.split(mod[bp:], 6, axis=-1)]

        qt, k, v, kb, vt, xl, gl = _in_proj(yp, mods_p[1], mods_p[0], g_norm1[l], w_in_bf,
                                            1, tt_p, da_w, lru_w, tk=tq)
        o = _prompt_attention(qt, kb.reshape(bp, seq, da_w), vt, lamp, g_subln[l], slopes, lam0, tq)
        y_lru, cp, hp = _lru_prompt(xl, gl, conv_w[l], conv_b[l], wg_bf, bg, lru_lambda[l],
                                    bp, seq, tm_lru)
        yp = _mlp(yp, o.reshape(bp * seq, da_w), y_lru, mods_p[2], mods_p[4], mods_p[3], mods_p[5],
                  g_norm2[l], g_final, w_out_bf, w1_bf, w2_bf, 1, tt_p, final)
        kp_l.append(k.reshape(bp, seq, N_DA_HEADS, dv))
        vp_l.append(v.reshape(bp, seq, N_DA_HEADS, dv))
        cp_l.append(cp)
        hp_l.append(hp.reshape(bp, lru_w))

        q, k, v, xl, gl = _in_proj(ys, mods_s[1], mods_s[0], g_norm1[l], w_in_bf,
                                   bb_s, t_new, da_w, lru_w)
        o = _sample_attention(q.reshape(bs, t_new, da_w),
                              k.reshape(bs, t_new * N_DA_HEADS, dv),
                              v.reshape(bs, t_new * N_DA_HEADS, dv),
                              cache_k2, cache_v2, page_table, l * n_pool, lamp, g_subln[l],
                              lam0, n_pg)
        y_lru, cs, hs = _lru_sample(xl, gl, state_conv[l], state_h[l], conv_w[l], conv_b[l],
                                    wg_bf, bg, lru_lambda[l], t_new)
        ys = _mlp(ys, o.reshape(bs * t_new, da_w), y_lru, mods_s[2], mods_s[4], mods_s[3],
                  mods_s[5], g_norm2[l], g_final, w_out_bf, w1_bf, w2_bf, bb_s, t_new, final)
        ks_l.append(k.reshape(bs, t_new, N_DA_HEADS, dv))
        vs_l.append(v.reshape(bs, t_new, N_DA_HEADS, dv))
        cs_l.append(cs.reshape(bs, hist, lru_w))
        hs_l.append(hs)

    return (yp, ys,
            jnp.stack(kp_l), jnp.stack(vp_l), jnp.stack(cp_l), jnp.stack(hp_l),
            jnp.stack(ks_l), jnp.stack(vs_l), jnp.stack(cs_l), jnp.stack(hs_l))
```

```python
import functools
import math

import jax
import jax.numpy as jnp
import numpy as np
from jax import lax
from jax.experimental import pallas as pl
from jax.experimental.pallas import tpu as pltpu

F32 = jnp.float32
BF16 = jnp.bfloat16

N_DA_HEADS = 4
N_LRU_HEADS = 8
CONV_WIDTH = 4
LRU_C = 8.0
EPS = 1e-6
NEG_BIG = -1e30
SUBLANES = 8
LANES = 128
VMEM_LIMIT_BYTES = 56 * 1024 * 1024
SAMPLE_RING_SLOTS = 3
LRU_SEGMENTS = SUBLANES
LRU_SEG_PAD = SUBLANES


def _lambda_init(layer):
    return 0.8 - 0.6 * math.exp(-0.3 * layer)


def _params(*sem):
    return pltpu.CompilerParams(dimension_semantics=sem, vmem_limit_bytes=VMEM_LIMIT_BYTES)


def _rms(x, g):
    return x * lax.rsqrt(jnp.mean(x * x, axis=-1, keepdims=True) + EPS) * g


def _ada_kernel(c_ref, w_ref, b_ref, o_ref):
    c = c_ref[...]
    s = (c * jax.nn.sigmoid(c)).astype(BF16)
    o_ref[...] = jnp.dot(s, w_ref[...].astype(BF16), preferred_element_type=F32) + b_ref[...]


def _ada_mod(c, w, b, tn=1536):
    m, d = c.shape
    n = w.shape[1]
    return pl.pallas_call(
        _ada_kernel,
        grid=(n // tn,),
        in_specs=[pl.BlockSpec((m, d), lambda j: (0, 0)),
                  pl.BlockSpec((d, tn), lambda j: (0, j)),
                  pl.BlockSpec((1, tn), lambda j: (0, j))],
        out_specs=pl.BlockSpec((m, tn), lambda j: (0, j)),
        out_shape=jax.ShapeDtypeStruct((m, n), F32),
        compiler_params=_params("arbitrary"),
        name="ada_mod",
    )(c, w, b.reshape(1, n))


def _inproj_kernel(x_ref, sc_ref, sh_ref, g_ref, w_ref, *out_refs, da_w, lru_w, q_scale, tk):
    bb, tt, d = x_ref.shape
    rows = bb * tt
    dv = da_w // N_DA_HEADS
    h = _rms(x_ref[...], g_ref[...]) * (1.0 + sc_ref[...]) + sh_ref[...]
    h = h.reshape(rows, d).astype(BF16)
    proj = jnp.dot(h, w_ref[...], preferred_element_type=F32)
    q = proj[:, :da_w] * q_scale
    k = proj[:, da_w:2 * da_w]
    v = proj[:, 2 * da_w:3 * da_w]
    if tk is None:
        q_ref, k4_ref, v4_ref, xl_ref, gl_ref = out_refs
        q_ref[...] = q.astype(BF16)
    else:
        qt_ref, k4_ref, v4_ref, kb_ref, vt_ref, xl_ref, gl_ref = out_refs
        qt_ref[...] = q.T.astype(BF16)
        kb_ref[...] = k.astype(BF16)
        vt = v.T.astype(BF16)
        for c in range(rows // tk):
            vt_ref[c] = vt[:, c * tk:(c + 1) * tk]
    for hh in range(N_DA_HEADS):
        k4_ref[pl.ds(hh, rows, stride=N_DA_HEADS), :] = k[:, hh * dv:(hh + 1) * dv]
        v4_ref[pl.ds(hh, rows, stride=N_DA_HEADS), :] = v[:, hh * dv:(hh + 1) * dv]
    xl_ref[...] = proj[:, 3 * da_w:3 * da_w + lru_w]
    gl_ref[...] = proj[:, 3 * da_w + lru_w:]


def _in_proj(x, sc, sh, g, w_bf, bb, tt, da_w, lru_w, tk=None):
    b, t, d = x.shape
    n_tok = b * t
    nt = t // tt
    rows = bb * tt
    dv = da_w // N_DA_HEADS
    dqk = dv // 2

    def flat(width, dtype, mult=1):
        return (pl.BlockSpec((rows * mult, width), lambda i, j: (i * nt + j, 0)),
                jax.ShapeDtypeStruct((n_tok * mult, width), dtype))

    kv4 = [flat(dv, F32, N_DA_HEADS), flat(dv, F32, N_DA_HEADS)]
    lru = [flat(lru_w, F32), flat(lru_w, F32)]
    if tk is None:
        outs = [flat(da_w, BF16)] + kv4 + lru
    else:
        assert bb == 1 and tt % tk == 0
        qt = (pl.BlockSpec((None, da_w, tt), lambda i, j: (i, 0, j)),
              jax.ShapeDtypeStruct((b, da_w, t), BF16))
        vt = (pl.BlockSpec((None, tt // tk, da_w, tk), lambda i, j: (i, j, 0, 0)),
              jax.ShapeDtypeStruct((b, t // tk, da_w, tk), BF16))
        outs = [qt] + kv4 + [flat(da_w, BF16), vt] + lru
    return pl.pallas_call(
        functools.partial(_inproj_kernel, da_w=da_w, lru_w=lru_w, q_scale=dqk ** -0.5, tk=tk),
        grid=(b // bb, nt),
        in_specs=[pl.BlockSpec((bb, tt, d), lambda i, j: (i, j, 0)),
                  pl.BlockSpec((bb, 1, d), lambda i, j: (i, 0, 0)),
                  pl.BlockSpec((bb, 1, d), lambda i, j: (i, 0, 0)),
                  pl.BlockSpec((1, 1, d), lambda i, j: (0, 0, 0)),
                  pl.BlockSpec(w_bf.shape, lambda i, j: (0, 0))],
        out_specs=[o[0] for o in outs],
        out_shape=[o[1] for o in outs],
        compiler_params=_params("arbitrary", "arbitrary"),
        name="in_proj",
    )(x, sc, sh, g.reshape(1, 1, d), w_bf)


def _lam_value(lamp_ref, lam0):
    lp = lamp_ref[...]
    t1 = jnp.sum(lp[0:1] * lp[1:2], axis=-1, keepdims=True)
    t2 = jnp.sum(lp[2:3] * lp[3:4], axis=-1, keepdims=True)
    return jnp.exp(t1) - jnp.exp(t2) + lam0


def _softmax_update(s, m_sc, l_sc, acc_sc, pv_fn):
    m_old = m_sc[...]
    m_new = jnp.maximum(m_old, jnp.max(s, axis=-1, keepdims=True))
    alpha = jnp.exp(m_old - m_new)
    p = jnp.exp(s - m_new)
    l_sc[...] = alpha * l_sc[...] + jnp.sum(p, axis=-1, keepdims=True)
    acc_sc[...] = alpha * acc_sc[...] + pv_fn(p.astype(BF16))
    m_sc[...] = m_new


def _nt_dot(a, b):
    return lax.dot_general(a, b, (((1,), (1,)), ((), ())), preferred_element_type=F32)


WORK_COLS = 6


def _prompt_work_table(n_steps, n_batch, n_qt):
    units = [(b, qi, j) for b in range(n_batch) for qi in range(n_qt) for j in range(qi + 1)]
    assert len(units) <= n_steps, "prompt attention units must fit the sample-attention grid"
    step_of = {u * n_steps // len(units): u for u in range(len(units))}
    table = np.zeros((n_steps, WORK_COLS), np.int32)
    nxt, done = 0, None
    for g in range(n_steps):
        u = step_of.get(g)
        if u is not None:
            nxt, done = u + 1, u
        b_in, qi_in, j_in = units[u if u is not None else min(nxt, len(units) - 1)]
        b_out, qi_out, _ = units[done if done is not None else 0]
        table[g] = (u is not None, b_in, qi_in, j_in, b_out, qi_out)
    return table.reshape(-1)


def _prompt_unit(qi, j, slopes_ref, qt_ref, k_ref, vt_ref, lamp_ref, g_ref, o_ref,
                 m_sc, l_sc, acc_sc, s_sc, qst_sc, *, tq, lam0):
    dv = g_ref.shape[0]
    heads = range(N_DA_HEADS)
    cols = [slice(h * dv, (h + 1) * dv) for h in heads]
    key = lax.broadcasted_iota(jnp.int32, (tq, LANES), 0).astype(F32)

    def scores(jj, slot):
        start = pl.multiple_of(jj * tq, tq)
        for h in heads:
            s_sc[slot, h] = jnp.dot(k_ref[pl.ds(start, tq), cols[h]], qst_sc[h],
                                    preferred_element_type=F32)

    def update(slot, masked):
        if masked:
            kk = lax.broadcasted_iota(jnp.int32, (tq, 2 * tq), 0)
            qq = lax.broadcasted_iota(jnp.int32, (tq, 2 * tq), 1)
            visible = kk <= jnp.where(qq >= tq, qq - tq, qq)
        tile_off = ((j - qi) * tq).astype(F32)
        for h in heads:
            s = s_sc[slot, h] + jnp.concatenate([slopes_ref[h] * key] * (2 * tq // LANES), axis=1)
            if masked:
                s = jnp.where(visible, s, NEG_BIG)
            shift = slopes_ref[h] * tile_off
            m_old = m_sc[h]
            m_new = jnp.maximum(m_old, jnp.max(s, axis=0, keepdims=True) + shift)
            alpha = jnp.exp(m_old - m_new)
            p = jnp.exp(s - (m_new - shift))
            l_sc[h] = alpha * l_sc[h] + jnp.sum(p, axis=0, keepdims=True)
            acc_sc[h] = alpha * acc_sc[h] + jnp.dot(vt_ref[j, cols[h], :], p.astype(BF16),
                                                    preferred_element_type=F32)
            m_sc[h] = m_new

    @pl.when(j == 0)
    def _start():
        row = lax.broadcasted_iota(jnp.int32, (dv, tq), 0)
        zero = jnp.zeros((dv, tq), BF16)
        for h in heads:
            qt = qt_ref[cols[h], :]
            qst_sc[h] = jnp.concatenate([jnp.where(row < dv // 2, qt, zero),
                                         jnp.where(row >= dv // 2, qt, zero)], axis=1)
        m_sc[...] = jnp.full(m_sc.shape, NEG_BIG, F32)
        l_sc[...] = jnp.zeros(l_sc.shape, F32)
        acc_sc[...] = jnp.zeros(acc_sc.shape, F32)
        scores(0, 0)

    for parity in (0, 1):
        @pl.when((j < qi) & (j % 2 == parity))
        def _off_diagonal():
            scores(j + 1, 1 - parity)
            update(parity, False)

        @pl.when((j == qi) & (j % 2 == parity))
        def _diagonal():
            update(parity, True)

    @pl.when(j == qi)
    def _finish():
        lam = _lam_value(lamp_ref, lam0)
        outs = []
        for h in heads:
            on = acc_sc[h] * (1.0 / l_sc[h])
            ot = on[:, :tq] - lam * on[:, tq:]
            ms = jnp.mean(ot * ot, axis=0, keepdims=True)
            outs.append((ot * lax.rsqrt(ms + EPS) * g_ref[...] * (1.0 - lam0)).T)
        o_ref[...] = jnp.concatenate(outs, axis=1).astype(o_ref.dtype)


def _attn_kernel(pt_ref, slopes_ref, work_ref,
                 q_ref, kn_ref, vn_ref, lamp_ref, g_ref, kc_hbm, vc_hbm,
                 qt_ref, kp_ref, vt_ref, gcol_ref,
                 o_ref, op_ref,
                 kbuf, vbuf, sem, m_sc, l_sc, acc_sc,
                 pm_sc, pl_sc, pacc_sc, ps_sc, qst_sc,
                 *, n_seq, nj, n_pg, pg_rows, past, lam0, pool_off, tq):
    seq = pl.program_id(0)
    j = pl.program_id(1)
    n_chunks = n_seq * nj
    n_slots = kbuf.shape[0]
    g = seq * nj + j
    t_new, da_w = q_ref.shape
    dv = da_w // N_DA_HEADS
    rows_h = 2 * t_new
    n_rows = N_DA_HEADS * rows_h
    n_keys = n_pg * pg_rows // N_DA_HEADS

    def chunk_copies(chunk, slot):
        cs = chunk // nj
        cj = chunk % nj
        copies = []
        for i in range(n_pg):
            page = pt_ref[cs, cj * n_pg + i] + pool_off
            rows = pl.ds(i * pg_rows, pg_rows)
            copies.append(pltpu.make_async_copy(kc_hbm.at[page], kbuf.at[slot, rows], sem.at[0, slot]))
            copies.append(pltpu.make_async_copy(vc_hbm.at[page], vbuf.at[slot, rows], sem.at[1, slot]))
        return copies

    @pl.when(g == 0)
    def _prologue():
        for c in range(min(n_slots - 1, n_chunks)):
            for cp in chunk_copies(c, c):
                cp.start()

    ahead = g + (n_slots - 1)

    @pl.when(ahead < n_chunks)
    def _prefetch():
        for cp in chunk_copies(ahead, ahead % n_slots):
            cp.start()

    @pl.when(work_ref[g * WORK_COLS] == 1)
    def _prompt():
        _prompt_unit(work_ref[g * WORK_COLS + 2], work_ref[g * WORK_COLS + 3], slopes_ref,
                     qt_ref, kp_ref, vt_ref, lamp_ref, gcol_ref, op_ref,
                     pm_sc, pl_sc, pacc_sc, ps_sc, qst_sc, tq=tq, lam0=lam0)

    slot = g % n_slots
    for cp in chunk_copies(g, slot):
        cp.wait()

    row1 = lax.broadcasted_iota(jnp.int32, (n_rows, 1), 0)
    h_row = row1 // rows_h
    q_row = row1 % t_new
    slope_row = jnp.exp2(-8.0 * (h_row + 1).astype(F32) / N_DA_HEADS)

    qf = q_ref[...].astype(F32)
    lane = lax.broadcasted_iota(jnp.int32, (t_new, dv), 1)
    pieces = []
    for h in range(N_DA_HEADS):
        qh = qf[:, h * dv:(h + 1) * dv]
        pieces.append(jnp.where(lane < dv // 2, qh, 0.0))
        pieces.append(jnp.where(lane >= dv // 2, qh, 0.0))
    qall = jnp.concatenate(pieces, axis=0).astype(BF16)

    @pl.when(j == 0)
    def _init():
        m_sc[...] = jnp.full(m_sc.shape, NEG_BIG, F32)
        l_sc[...] = jnp.zeros(l_sc.shape, F32)
        acc_sc[...] = jnp.zeros(acc_sc.shape, F32)

    def head_rows(buf, h):
        return buf[slot, pl.ds(h, n_keys, stride=N_DA_HEADS), :].astype(BF16)

    s = jnp.concatenate([_nt_dot(qall[h * rows_h:(h + 1) * rows_h], head_rows(kbuf, h))
                         for h in range(N_DA_HEADS)], axis=0)
    kpos = lax.broadcasted_iota(jnp.int32, (1, n_keys), 1) + (j * n_keys - past)
    s = s + slope_row * kpos.astype(F32)

    def pv(p):
        return jnp.concatenate(
            [jnp.dot(p[h * rows_h:(h + 1) * rows_h], head_rows(vbuf, h), preferred_element_type=F32)
             for h in range(N_DA_HEADS)], axis=0)

    _softmax_update(s, m_sc, l_sc, acc_sc, pv)

    @pl.when(j == nj - 1)
    def _finish():
        n_new = kn_ref.shape[0]
        pad = jnp.zeros((LANES - n_new, dv), F32)
        kn = jnp.concatenate([kn_ref[...], pad], axis=0).astype(BF16)
        vn = jnp.concatenate([vn_ref[...], pad], axis=0).astype(BF16)
        c = lax.broadcasted_iota(jnp.int32, (n_rows, LANES), 1)
        key = c // N_DA_HEADS
        ok = (c % N_DA_HEADS == h_row) & (key <= q_row) & (c < n_new)
        sn = jnp.where(ok, _nt_dot(qall, kn) + slope_row * key.astype(F32), NEG_BIG)
        _softmax_update(sn, m_sc, l_sc, acc_sc,
                        lambda p: jnp.dot(p, vn, preferred_element_type=F32))

        lam = _lam_value(lamp_ref, lam0)
        on = acc_sc[...] / l_sc[...]
        outs = []
        for h in range(N_DA_HEADS):
            o1 = on[h * rows_h:h * rows_h + t_new]
            o2 = on[h * rows_h + t_new:(h + 1) * rows_h]
            outs.append(_rms(o1 - lam * o2, g_ref[...]) * (1.0 - lam0))
        o_ref[...] = jnp.concatenate(outs, axis=1).astype(o_ref.dtype)


def _attention(q, k_new, v_new, cache_k2, cache_v2, page_table, pool_off, qt, kb, vt, lamp, g_subln,
               slopes, lam0, n_pg, tq):
    b, t_new, da_w = q.shape
    bp, s, _ = kb.shape
    dv = da_w // N_DA_HEADS
    n_pages = page_table.shape[1]
    pg_rows = cache_k2.shape[1]
    past = n_pages * (pg_rows // N_DA_HEADS)
    n_rows = N_DA_HEADS * 2 * t_new
    nj = n_pages // n_pg
    work = jnp.asarray(_prompt_work_table(b * nj, bp, s // tq))
    kernel = functools.partial(_attn_kernel, n_seq=b, nj=nj, n_pg=n_pg, pg_rows=pg_rows, past=past,
                               lam0=lam0, pool_off=pool_off, tq=tq)

    def wk(col):
        return lambda i, j, pt, sl, w: w[(i * nj + j) * WORK_COLS + col]

    b_in, qi_in, b_out, qi_out = wk(1), wk(2), wk(4), wk(5)
    seq_blk = lambda i, j, pt, sl, w: (i, 0, 0)
    const2 = lambda i, j, pt, sl, w: (0, 0)
    in_specs = [pl.BlockSpec((None, t_new, da_w), seq_blk),
                pl.BlockSpec((None, t_new * N_DA_HEADS, dv), seq_blk),
                pl.BlockSpec((None, t_new * N_DA_HEADS, dv), seq_blk),
                pl.BlockSpec(lamp.shape, const2),
                pl.BlockSpec((1, dv), const2),
                pl.BlockSpec(memory_space=pl.ANY),
                pl.BlockSpec(memory_space=pl.ANY),
                pl.BlockSpec((None, da_w, tq), lambda *a: (b_in(*a), 0, qi_in(*a))),
                pl.BlockSpec((None, s, da_w), lambda *a: (b_in(*a), 0, 0)),
                pl.BlockSpec((None, s // tq, da_w, tq), lambda *a: (b_in(*a), 0, 0, 0)),
                pl.BlockSpec((dv, 1), const2)]
    out_specs = [pl.BlockSpec((None, t_new, da_w), seq_blk),
                 pl.BlockSpec((None, tq, da_w), lambda *a: (b_out(*a), qi_out(*a), 0))]
    ring = pltpu.VMEM((SAMPLE_RING_SLOTS, n_pg * pg_rows, dv), F32)
    return pl.pallas_call(
        kernel,
        grid_spec=pltpu.PrefetchScalarGridSpec(
            num_scalar_prefetch=3,
            grid=(b, nj),
            in_specs=in_specs,
            out_specs=out_specs,
            scratch_shapes=[ring, ring, pltpu.SemaphoreType.DMA((2, SAMPLE_RING_SLOTS)),
                            pltpu.VMEM((n_rows, 1), F32), pltpu.VMEM((n_rows, 1), F32),
                            pltpu.VMEM((n_rows, dv), F32),
                            pltpu.VMEM((N_DA_HEADS, 1, 2 * tq), F32),
                            pltpu.VMEM((N_DA_HEADS, 1, 2 * tq), F32),
                            pltpu.VMEM((N_DA_HEADS, dv, 2 * tq), F32),
                            pltpu.VMEM((2, N_DA_HEADS, tq, 2 * tq), F32),
                            pltpu.VMEM((N_DA_HEADS, dv, 2 * tq), BF16)]),
        out_shape=[jax.ShapeDtypeStruct((b, t_new, da_w), BF16),
                   jax.ShapeDtypeStruct((bp, s, da_w), BF16)],
        compiler_params=_params("arbitrary", "arbitrary"),
        name="attention",
    )(page_table, slopes, work, q, k_new, v_new, lamp, g_subln.reshape(1, dv), cache_k2, cache_v2,
      qt, kb, vt, g_subln.reshape(dv, 1))


def _softplus(z):
    return jnp.maximum(z, 0.0) + jnp.log1p(jnp.exp(-jnp.abs(z)))


def _gelu_tanh(x):
    return 0.5 * x * (1.0 + jnp.tanh(math.sqrt(2.0 / math.pi) * (x + 0.044715 * (x * x * x))))


def _lru_gates(xc, wg_ref, bg_ref, lam_ref):
    w = xc.shape[-1]
    g = jnp.dot(xc.astype(BF16), wg_ref[...], preferred_element_type=F32) + bg_ref[...]
    r = jax.nn.sigmoid(g[:, :w])
    ig = jax.nn.sigmoid(g[:, w:])
    log_a = -LRU_C * r * _softplus(-lam_ref[...])
    a = jnp.exp(log_a)
    u = jnp.sqrt(-jnp.tanh(log_a) * (1.0 + a * a)) * (ig * xc)
    return a, u


def _load_blocked(sc, rows):
    return jnp.concatenate([sc[c, rows, :] for c in range(sc.shape[0])], axis=1)


def _store_blocked(sc, rows, val):
    for c in range(sc.shape[0]):
        sc[c, rows, :] = val[:, c * LANES:(c + 1) * LANES]


def _blocked(rows, w):
    return pltpu.VMEM((w // LANES, rows, LANES), F32)


def _lru_prompt_kernel(xl_ref, gl_ref, cw_ref, cb_ref, wg_ref, bg_ref, lam_ref,
                       y_ref, conv_ref, hlast_ref, xbuf, a_sc, u_sc, hcar):
    t = pl.program_id(1)
    nt = pl.num_programs(1)
    tm, w = xl_ref.shape
    seg = tm // LRU_SEGMENTS
    pitch = seg + LRU_SEG_PAD
    hist = CONV_WIDTH - 1

    @pl.when(t == 0)
    def _init():
        xbuf[0:SUBLANES, :] = jnp.zeros((SUBLANES, w), F32)
        hcar[...] = jnp.zeros(hcar.shape, F32)

    xbuf[SUBLANES:SUBLANES + tm, :] = xl_ref[...]
    xc = cb_ref[...]
    for jj in range(CONV_WIDTH):
        xc = xc + xbuf[pl.ds(SUBLANES - hist + jj, tm), :] * cw_ref[jj:jj + 1, :]
    a, u = _lru_gates(xc, wg_ref, bg_ref, lam_ref)
    for s in range(LRU_SEGMENTS):
        _store_blocked(a_sc, slice(s * pitch, s * pitch + seg), a[s * seg:(s + 1) * seg])
        _store_blocked(u_sc, slice(s * pitch, s * pitch + seg), u[s * seg:(s + 1) * seg])

    def step(i, carry):
        p, hh = carry
        rows = pl.ds(i, LRU_SEGMENTS, stride=pitch)
        ai = _load_blocked(a_sc, rows)
        p = ai * p
        hh = ai * hh + _load_blocked(u_sc, rows)
        _store_blocked(a_sc, rows, p)
        _store_blocked(u_sc, rows, hh)
        return p, hh

    p_end, h_end = lax.fori_loop(0, seg, step,
                                 (jnp.ones((LRU_SEGMENTS, w), F32), jnp.zeros((LRU_SEGMENTS, w), F32)))

    h_in = hcar[...]
    for s in range(LRU_SEGMENTS):
        blk = slice(s * seg, (s + 1) * seg)
        sblk = slice(s * pitch, s * pitch + seg)
        hs = _load_blocked(u_sc, sblk) + _load_blocked(a_sc, sblk) * h_in
        y_ref[blk, :] = (hs * _gelu_tanh(gl_ref[blk, :])).astype(y_ref.dtype)
        h_in = p_end[s:s + 1, :] * h_in + h_end[s:s + 1, :]
    hcar[...] = h_in
    xbuf[0:SUBLANES, :] = xbuf[tm:tm + SUBLANES, :]

    @pl.when(t == nt - 1)
    def _fin():
        hlast_ref[...] = h_in
        conv_ref[...] = xbuf[pl.ds(SUBLANES - hist, hist), :]


def _lru_prompt(xl, gl, conv_w, conv_b, wg_bf, bg, lru_lambda, b, s, tm):
    w = xl.shape[-1]
    nt = s // tm
    hist = CONV_WIDTH - 1
    scan_rows = LRU_SEGMENTS * (tm // LRU_SEGMENTS + LRU_SEG_PAD)
    full = lambda shape: pl.BlockSpec(shape, lambda i, j: (0,) * len(shape))
    return pl.pallas_call(
        _lru_prompt_kernel,
        grid=(b, nt),
        in_specs=[pl.BlockSpec((tm, w), lambda i, j: (i * nt + j, 0)),
                  pl.BlockSpec((tm, w), lambda i, j: (i * nt + j, 0)),
                  full((CONV_WIDTH, w)), full((1, w)), full(wg_bf.shape), full((1, 2 * w)),
                  full((1, w))],
        out_specs=[pl.BlockSpec((tm, w), lambda i, j: (i * nt + j, 0)),
                   pl.BlockSpec((None, hist, w), lambda i, j: (i, 0, 0)),
                   pl.BlockSpec((None, 1, w), lambda i, j: (i, 0, 0))],
        out_shape=[jax.ShapeDtypeStruct((b * s, w), BF16),
                   jax.ShapeDtypeStruct((b, hist, w), F32),
                   jax.ShapeDtypeStruct((b, 1, w), F32)],
        scratch_shapes=[pltpu.VMEM((tm + SUBLANES, w), F32), _blocked(scan_rows, w),
                        _blocked(scan_rows, w), pltpu.VMEM((1, w), F32)],
        compiler_params=_params("arbitrary", "arbitrary"),
        name="lru_prompt",
    )(xl, gl, conv_w, conv_b.reshape(1, w), wg_bf, bg.reshape(1, 2 * w), lru_lambda.reshape(1, w))


def _lru_sample_kernel(xl_ref, gl_ref, cbuf_ref, h0_ref, cw_ref, cb_ref, wg_ref, bg_ref, lam_ref,
                       y_ref, conv_ref, hlast_ref, x_sc, g_sc, c_sc, y_sc, *, t_new):
    nb = h0_ref.shape[0]
    hist = CONV_WIDTH - 1
    _store_blocked(x_sc, slice(None), xl_ref[...])
    _store_blocked(g_sc, slice(None), gl_ref[...])
    _store_blocked(c_sc, slice(None), cbuf_ref[...])
    xp = [_load_blocked(c_sc, pl.ds(jj, nb, stride=hist)) for jj in range(hist)]
    xp += [_load_blocked(x_sc, pl.ds(tt, nb, stride=t_new)) for tt in range(t_new)]
    hh = h0_ref[...]
    for tt in range(t_new):
        xc = cb_ref[...]
        for jj in range(CONV_WIDTH):
            xc = xc + xp[tt + jj] * cw_ref[jj:jj + 1, :]
        a, u = _lru_gates(xc, wg_ref, bg_ref, lam_ref)
        hh = a * hh + u
        gate = _gelu_tanh(_load_blocked(g_sc, pl.ds(tt, nb, stride=t_new)))
        _store_blocked(y_sc, pl.ds(tt, nb, stride=t_new), hh * gate)
    hlast_ref[...] = hh
    y_ref[...] = _load_blocked(y_sc, slice(None)).astype(y_ref.dtype)
    for jj in range(hist):
        _store_blocked(c_sc, pl.ds(jj, nb, stride=hist), xp[t_new + jj])
    conv_ref[...] = _load_blocked(c_sc, slice(None))


def _lru_sample(xl, gl, conv_buf, h0, conv_w, conv_b, wg_bf, bg, lru_lambda, t_new):
    n_tok, w = xl.shape
    nb = n_tok // t_new
    hist = CONV_WIDTH - 1
    full = lambda shape: pl.BlockSpec(shape, lambda i: (0,) * len(shape))
    return pl.pallas_call(
        functools.partial(_lru_sample_kernel, t_new=t_new),
        grid=(1,),
        in_specs=[full((n_tok, w)), full((n_tok, w)), full((nb * hist, w)), full((nb, w)),
                  full((CONV_WIDTH, w)), full((1, w)), full(wg_bf.shape), full((1, 2 * w)),
                  full((1, w))],
        out_specs=[full((n_tok, w)), full((nb * hist, w)), full((nb, w))],
        out_shape=[jax.ShapeDtypeStruct((n_tok, w), BF16),
                   jax.ShapeDtypeStruct((nb * hist, w), F32),
                   jax.ShapeDtypeStruct((nb, w), F32)],
        scratch_shapes=[_blocked(n_tok, w), _blocked(n_tok, w), _blocked(nb * hist, w),
                        _blocked(n_tok, w)],
        compiler_params=_params("arbitrary"),
        name="lru_sample",
    )(xl, gl, conv_buf.reshape(nb * hist, w), h0, conv_w, conv_b.reshape(1, w), wg_bf,
      bg.reshape(1, 2 * w), lru_lambda.reshape(1, w))


def _mlp_kernel(x_ref, o_ref, y_ref, gt1_ref, sc2_ref, sh2_ref, gt2_ref, g2_ref, gf_ref,
                wout_ref, w1_ref, w2_ref, out_ref, *, ff_chunk, final_norm):
    bb, tt, d = x_ref.shape
    rows = bb * tt
    mix_in = jnp.concatenate([o_ref[...], y_ref[...]], axis=1)
    mix = jnp.dot(mix_in, wout_ref[...], preferred_element_type=F32).reshape(bb, tt, d)
    x1 = x_ref[...] + gt1_ref[...] * mix
    h2 = (_rms(x1, g2_ref[...]) * (1.0 + sc2_ref[...]) + sh2_ref[...]).reshape(rows, d).astype(BF16)
    d_ff = w1_ref.shape[1]
    ff = jnp.zeros((rows, d), F32)
    for c in range(d_ff // ff_chunk):
        cs = slice(c * ff_chunk, (c + 1) * ff_chunk)
        hc = jnp.dot(h2, w1_ref[:, cs], preferred_element_type=F32)
        hc = jnp.square(jnp.maximum(hc, 0.0)).astype(BF16)
        ff = ff + jnp.dot(hc, w2_ref[cs, :], preferred_element_type=F32)
    x2 = x1 + gt2_ref[...] * ff.reshape(bb, tt, d)
    out_ref[...] = _rms(x2, gf_ref[...]) if final_norm else x2


def _mlp(x, o, y, gt1, sc2, sh2, gt2, g2, gf, wout_bf, w1_bf, w2_bf, bb, tt, final_norm):
    b, t, d = x.shape
    nt = t // tt
    rows = bb * tt
    mix_w = o.shape[-1]
    mod = pl.BlockSpec((bb, 1, d), lambda i, j: (i, 0, 0))
    gain = pl.BlockSpec((1, 1, d), lambda i, j: (0, 0, 0))
    wspec = lambda w: pl.BlockSpec(w.shape, lambda i, j: (0, 0), pipeline_mode=pl.Buffered(1))
    return pl.pallas_call(
        functools.partial(_mlp_kernel, ff_chunk=min(1024, w1_bf.shape[1]), final_norm=final_norm),
        grid=(b // bb, nt),
        in_specs=[pl.BlockSpec((bb, tt, d), lambda i, j: (i, j, 0)),
                  pl.BlockSpec((rows, mix_w), lambda i, j: (i * nt + j, 0)),
                  pl.BlockSpec((rows, y.shape[-1]), lambda i, j: (i * nt + j, 0)),
                  mod, mod, mod, mod, gain, gain, wspec(wout_bf), wspec(w1_bf), wspec(w2_bf)],
        out_specs=pl.BlockSpec((bb, tt, d), lambda i, j: (i, j, 0)),
        out_shape=jax.ShapeDtypeStruct((b, t, d), F32),
        compiler_params=_params("arbitrary", "arbitrary"),
        name="out_mlp",
    )(x, o, y, gt1, sc2, sh2, gt2, g2.reshape(1, 1, d), gf.reshape(1, 1, d), wout_bf, w1_bf, w2_bf)


def _block_diag(w):
    h, i, j = w.shape
    eye = jnp.eye(h, dtype=w.dtype)
    return (eye[:, None, :, None] * w[:, :, None, :]).reshape(h * i, h * j)


def _pick(n, target):
    t = min(n, target)
    while n % t:
        t -= 1
    return t


def kernel(x_prompt, x_sample, c_prompt, c_sample, cache_k, cache_v, page_table, state_h, state_conv, w_ada, b_ada, g_norm1, g_norm2, w_in, lambda_q1, lambda_k1, lambda_q2, lambda_k2, g_subln, conv_w, conv_b, w_rg, b_rg, w_ig, b_ig, lru_lambda, w_out, w_ff1, w_ff2, g_final):
    depth = w_in.shape[0]
    bp, seq, d = x_prompt.shape
    bs, t_new, _ = x_sample.shape
    lru_w = conv_w.shape[-1]
    da_w = (w_in.shape[-1] - 2 * lru_w) // 3
    dv = da_w // N_DA_HEADS
    n_pool, page_size = cache_k.shape[1], cache_k.shape[2]
    n_pages = page_table.shape[1]
    hist = CONV_WIDTH - 1

    slopes = 2.0 ** (-8.0 * jnp.arange(1, N_DA_HEADS + 1, dtype=F32) / N_DA_HEADS)
    cache_k2 = cache_k.reshape(depth * n_pool, page_size * N_DA_HEADS, dv)
    cache_v2 = cache_v.reshape(depth * n_pool, page_size * N_DA_HEADS, dv)

    tt_p = _pick(seq, 512)
    bb_s = _pick(bs, 64)
    tq = _pick(seq, 256)
    n_pg = _pick(n_pages, 16)
    tm_lru = _pick(seq, 512)

    yp, ys = x_prompt, x_sample
    kp_l, vp_l, cp_l, hp_l, ks_l, vs_l, cs_l, hs_l = [], [], [], [], [], [], [], []
    for l in range(depth):
        lam0 = _lambda_init(l)
        w_in_bf = w_in[l].astype(BF16)
        w_out_bf = w_out[l].astype(BF16)
        w1_bf = w_ff1[l].astype(BF16)
        w2_bf = w_ff2[l].astype(BF16)
        wg_bf = jnp.concatenate([_block_diag(w_rg[l]), _block_diag(w_ig[l])], axis=1).astype(BF16)
        bg = jnp.concatenate([b_rg[l], b_ig[l]])
        lamp = jnp.stack([lambda_q1[l], lambda_k1[l], lambda_q2[l], lambda_k2[l]])
        final = l == depth - 1

        mod = _ada_mod(jnp.concatenate([c_prompt, c_sample], axis=0), w_ada[l], b_ada[l])
        mods_p = [m[:, None, :] for m in jnp.split(mod[:bp], 6, axis=-1)]
        mods_s = [m[:, None, :] for m in jnp.split(mod[bp:], 6, axis=-1)]

        qt, kp, vp, kb, vt, xl_p, gl_p = _in_proj(yp, mods_p[1], mods_p[0], g_norm1[l], w_in_bf,
                                                  1, tt_p, da_w, lru_w, tk=tq)
        q, ks, vs, xl_s, gl_s = _in_proj(ys, mods_s[1], mods_s[0], g_norm1[l], w_in_bf,
                                         bb_s, t_new, da_w, lru_w)
        o_s, o_p = _attention(q.reshape(bs, t_new, da_w),
                              ks.reshape(bs, t_new * N_DA_HEADS, dv),
                              vs.reshape(bs, t_new * N_DA_HEADS, dv),
                              cache_k2, cache_v2, page_table, l * n_pool,
                              qt, kb.reshape(bp, seq, da_w), vt, lamp, g_subln[l], slopes, lam0,
                              n_pg, tq)

        y_lru, cp, hp = _lru_prompt(xl_p, gl_p, conv_w[l], conv_b[l], wg_bf, bg, lru_lambda[l],
                                    bp, seq, tm_lru)
        yp = _mlp(yp, o_p.reshape(bp * seq, da_w), y_lru, mods_p[2], mods_p[4], mods_p[3],
                  mods_p[5], g_norm2[l], g_final, w_out_bf, w1_bf, w2_bf, 1, tt_p, final)
        kp_l.append(kp.reshape(bp, seq, N_DA_HEADS, dv))
        vp_l.append(vp.reshape(bp, seq, N_DA_HEADS, dv))
        cp_l.append(cp)
        hp_l.append(hp.reshape(bp, lru_w))

        y_lru, cs, hs = _lru_sample(xl_s, gl_s, state_conv[l], state_h[l], conv_w[l], conv_b[l],
                                    wg_bf, bg, lru_lambda[l], t_new)
        ys = _mlp(ys, o_s.reshape(bs * t_new, da_w), y_lru, mods_s[2], mods_s[4], mods_s[3],
                  mods_s[5], g_norm2[l], g_final, w_out_bf, w1_bf, w2_bf, bb_s, t_new, final)
        ks_l.append(ks.reshape(bs, t_new, N_DA_HEADS, dv))
        vs_l.append(vs.reshape(bs, t_new, N_DA_HEADS, dv))
        cs_l.append(cs.reshape(bs, hist, lru_w))
        hs_l.append(hs)

    return (yp, ys,
            jnp.stack(kp_l), jnp.stack(vp_l), jnp.stack(cp_l), jnp.stack(hp_l),
            jnp.stack(ks_l), jnp.stack(vs_l), jnp.stack(cs_l), jnp.stack(hs_l))
```

```python
import functools
import math

import jax
import jax.numpy as jnp
import numpy as np
from jax import lax
from jax.experimental import pallas as pl
from jax.experimental.pallas import tpu as pltpu

F32 = jnp.float32
BF16 = jnp.bfloat16

N_DA_HEADS = 4
N_LRU_HEADS = 8
CONV_WIDTH = 4
LRU_C = 8.0
EPS = 1e-6
NEG_BIG = -1e30
SUBLANES = 8
LANES = 128
MXU_DIM = 256
VMEM_LIMIT_BYTES = 56 * 1024 * 1024
SAMPLE_RING_SLOTS = 3
LRU_SEGMENTS = SUBLANES
LRU_SEG_PAD = SUBLANES


def _lambda_init(layer):
    return 0.8 - 0.6 * math.exp(-0.3 * layer)


def _params(*sem):
    return pltpu.CompilerParams(dimension_semantics=sem, vmem_limit_bytes=VMEM_LIMIT_BYTES)


def _rms(x, g):
    return x * lax.rsqrt(jnp.mean(x * x, axis=-1, keepdims=True) + EPS) * g


def _ada_kernel(c_ref, w_ref, b_ref, o_ref):
    c = c_ref[...]
    s = (c * jax.nn.sigmoid(c)).astype(BF16)
    o_ref[...] = jnp.dot(s, w_ref[...].astype(BF16), preferred_element_type=F32) + b_ref[...]


def _ada_mod(c, w, b, tn=1536):
    m, d = c.shape
    n = w.shape[1]
    return pl.pallas_call(
        _ada_kernel,
        grid=(n // tn,),
        in_specs=[pl.BlockSpec((m, d), lambda j: (0, 0)),
                  pl.BlockSpec((d, tn), lambda j: (0, j)),
                  pl.BlockSpec((1, tn), lambda j: (0, j))],
        out_specs=pl.BlockSpec((m, tn), lambda j: (0, j)),
        out_shape=jax.ShapeDtypeStruct((m, n), F32),
        compiler_params=_params("arbitrary"),
        name="ada_mod",
    )(c, w, b.reshape(1, n))


def _inproj_kernel(x_ref, sc_ref, sh_ref, g_ref, w_ref, *out_refs, da_w, lru_w, q_scale, tk):
    bb, tt, d = x_ref.shape
    rows = bb * tt
    dv = da_w // N_DA_HEADS
    h = _rms(x_ref[...], g_ref[...]) * (1.0 + sc_ref[...]) + sh_ref[...]
    h = h.reshape(rows, d).astype(BF16)
    proj = jnp.dot(h, w_ref[...], preferred_element_type=F32)
    q = proj[:, :da_w] * q_scale
    k = proj[:, da_w:2 * da_w]
    v = proj[:, 2 * da_w:3 * da_w]
    if tk is None:
        q_ref, k4_ref, v4_ref, xl_ref, gl_ref = out_refs
        q_ref[...] = q.astype(BF16)
    else:
        qt_ref, k4_ref, v4_ref, kb_ref, vt_ref, xl_ref, gl_ref = out_refs
        qt_ref[...] = q.T.astype(BF16)
        kb_ref[...] = k.astype(BF16)
        vt = v.T.astype(BF16)
        for c in range(rows // tk):
            vt_ref[c] = vt[:, c * tk:(c + 1) * tk]
    for hh in range(N_DA_HEADS):
        k4_ref[pl.ds(hh, rows, stride=N_DA_HEADS), :] = k[:, hh * dv:(hh + 1) * dv]
        v4_ref[pl.ds(hh, rows, stride=N_DA_HEADS), :] = v[:, hh * dv:(hh + 1) * dv]
    xl_ref[...] = proj[:, 3 * da_w:3 * da_w + lru_w]
    gl_ref[...] = proj[:, 3 * da_w + lru_w:]


def _in_proj(x, sc, sh, g, w_bf, bb, tt, da_w, lru_w, tk=None):
    b, t, d = x.shape
    n_tok = b * t
    nt = t // tt
    rows = bb * tt
    dv = da_w // N_DA_HEADS
    dqk = dv // 2

    def flat(width, dtype, mult=1):
        return (pl.BlockSpec((rows * mult, width), lambda i, j: (i * nt + j, 0)),
                jax.ShapeDtypeStruct((n_tok * mult, width), dtype))

    kv4 = [flat(dv, F32, N_DA_HEADS), flat(dv, F32, N_DA_HEADS)]
    lru = [flat(lru_w, F32), flat(lru_w, F32)]
    if tk is None:
        outs = [flat(da_w, BF16)] + kv4 + lru
    else:
        assert bb == 1 and tt % tk == 0
        qt = (pl.BlockSpec((None, da_w, tt), lambda i, j: (i, 0, j)),
              jax.ShapeDtypeStruct((b, da_w, t), BF16))
        vt = (pl.BlockSpec((None, tt // tk, da_w, tk), lambda i, j: (i, j, 0, 0)),
              jax.ShapeDtypeStruct((b, t // tk, da_w, tk), BF16))
        outs = [qt] + kv4 + [flat(da_w, BF16), vt] + lru
    return pl.pallas_call(
        functools.partial(_inproj_kernel, da_w=da_w, lru_w=lru_w, q_scale=dqk ** -0.5, tk=tk),
        grid=(b // bb, nt),
        in_specs=[pl.BlockSpec((bb, tt, d), lambda i, j: (i, j, 0)),
                  pl.BlockSpec((bb, 1, d), lambda i, j: (i, 0, 0)),
                  pl.BlockSpec((bb, 1, d), lambda i, j: (i, 0, 0)),
                  pl.BlockSpec((1, 1, d), lambda i, j: (0, 0, 0)),
                  pl.BlockSpec(w_bf.shape, lambda i, j: (0, 0))],
        out_specs=[o[0] for o in outs],
        out_shape=[o[1] for o in outs],
        compiler_params=_params("arbitrary", "arbitrary"),
        name="in_proj",
    )(x, sc, sh, g.reshape(1, 1, d), w_bf)


def _lam_value(lamp_ref, lam0):
    lp = lamp_ref[...]
    t1 = jnp.sum(lp[0:1] * lp[1:2], axis=-1, keepdims=True)
    t2 = jnp.sum(lp[2:3] * lp[3:4], axis=-1, keepdims=True)
    return jnp.exp(t1) - jnp.exp(t2) + lam0


def _softmax_update(s, m_sc, l_sc, acc_sc, pv_fn):
    m_old = m_sc[...]
    m_new = jnp.maximum(m_old, jnp.max(s, axis=-1, keepdims=True))
    alpha = jnp.exp(m_old - m_new)
    p = jnp.exp(s - m_new)
    l_sc[...] = alpha * l_sc[...] + jnp.sum(p, axis=-1, keepdims=True)
    acc_sc[...] = alpha * acc_sc[...] + pv_fn(p.astype(BF16))
    m_sc[...] = m_new


def _nt_dot(a, b):
    return lax.dot_general(a, b, (((1,), (1,)), ((), ())), preferred_element_type=F32)


(WK_ATT_RUN, WK_ATT_B, WK_ATT_QI, WK_ATT_J, WK_ATT_OUT_B, WK_ATT_OUT_QI,
 WK_LRU_RUN, WK_LRU_TILE, WK_LRU_T, WK_LRU_OUT_TILE, WK_LRU_OUT_B, WORK_COLS) = range(12)


def _spread(n_steps, n_units, preferred):
    steps = preferred if len(preferred) >= n_units else list(range(n_steps))
    assert n_units <= len(steps), "prompt work units must fit the sample-attention grid"
    return {steps[u * len(steps) // n_units]: u for u in range(n_units)}


def _follow(n_steps, step_of):
    rows, nxt, done = [], 0, 0
    for g in range(n_steps):
        u = step_of.get(g)
        if u is not None:
            nxt, done = u + 1, u
        rows.append((u is not None, u if u is not None else min(nxt, len(step_of) - 1), done))
    return rows


def _prompt_work_table(n_steps, n_batch, n_qt, n_lru_t):
    att = [(b, qi, j) for b in range(n_batch) for qi in range(n_qt) for j in range(qi + 1)]
    lru = [(b, t) for b in range(n_batch) for t in range(n_lru_t)]
    att_steps = _spread(n_steps, len(att), list(range(n_steps)))
    lru_steps = _spread(n_steps, len(lru), [g for g in range(n_steps) if g not in att_steps])
    table = np.zeros((n_steps, WORK_COLS), np.int32)
    for g, (a, l) in enumerate(zip(_follow(n_steps, att_steps), _follow(n_steps, lru_steps))):
        table[g, WK_ATT_RUN], (table[g, WK_ATT_B], table[g, WK_ATT_QI], table[g, WK_ATT_J]) = a[0], att[a[1]]
        table[g, WK_ATT_OUT_B], table[g, WK_ATT_OUT_QI] = att[a[2]][:2]
        table[g, WK_LRU_RUN], table[g, WK_LRU_T] = l[0], lru[l[1]][1]
        table[g, WK_LRU_TILE] = lru[l[1]][0] * n_lru_t + lru[l[1]][1]
        table[g, WK_LRU_OUT_TILE] = lru[l[2]][0] * n_lru_t + lru[l[2]][1]
        table[g, WK_LRU_OUT_B] = lru[l[2]][0]
    return table.reshape(-1)


def _prompt_unit(qi, j, slopes_ref, qt_ref, k_ref, vt_ref, lamp_ref, g_ref, o_ref,
                 m_sc, l_sc, acc_sc, s_sc, qst_sc, *, tq, lam0):
    dv = g_ref.shape[0]
    heads = range(N_DA_HEADS)
    cols = [slice(h * dv, (h + 1) * dv) for h in heads]
    key = lax.broadcasted_iota(jnp.int32, (tq, LANES), 0).astype(F32)

    def scores(jj, slot):
        start = pl.multiple_of(jj * tq, tq)
        for h in heads:
            s_sc[slot, h] = jnp.dot(k_ref[pl.ds(start, tq), cols[h]], qst_sc[h],
                                    preferred_element_type=F32)

    def update(slot, masked):
        if masked:
            kk = lax.broadcasted_iota(jnp.int32, (tq, 2 * tq), 0)
            qq = lax.broadcasted_iota(jnp.int32, (tq, 2 * tq), 1)
            visible = kk <= jnp.where(qq >= tq, qq - tq, qq)
        tile_off = ((j - qi) * tq).astype(F32)
        for h in heads:
            s = s_sc[slot, h] + jnp.concatenate([slopes_ref[h] * key] * (2 * tq // LANES), axis=1)
            if masked:
                s = jnp.where(visible, s, NEG_BIG)
            shift = slopes_ref[h] * tile_off
            m_old = m_sc[h]
            m_new = jnp.maximum(m_old, jnp.max(s, axis=0, keepdims=True) + shift)
            alpha = jnp.exp(m_old - m_new)
            p = jnp.exp(s - (m_new - shift))
            l_sc[h] = alpha * l_sc[h] + jnp.sum(p, axis=0, keepdims=True)
            acc_sc[h] = alpha * acc_sc[h] + jnp.dot(vt_ref[j, cols[h], :], p.astype(BF16),
                                                    preferred_element_type=F32)
            m_sc[h] = m_new

    @pl.when(j == 0)
    def _start():
        row = lax.broadcasted_iota(jnp.int32, (dv, tq), 0)
        zero = jnp.zeros((dv, tq), BF16)
        for h in heads:
            qt = qt_ref[cols[h], :]
            qst_sc[h] = jnp.concatenate([jnp.where(row < dv // 2, qt, zero),
                                         jnp.where(row >= dv // 2, qt, zero)], axis=1)
        m_sc[...] = jnp.full(m_sc.shape, NEG_BIG, F32)
        l_sc[...] = jnp.zeros(l_sc.shape, F32)
        acc_sc[...] = jnp.zeros(acc_sc.shape, F32)
        scores(0, 0)

    for parity in (0, 1):
        @pl.when((j < qi) & (j % 2 == parity))
        def _off_diagonal():
            scores(j + 1, 1 - parity)
            update(parity, False)

        @pl.when((j == qi) & (j % 2 == parity))
        def _diagonal():
            update(parity, True)

    @pl.when(j == qi)
    def _finish():
        lam = _lam_value(lamp_ref, lam0)
        outs = []
        for h in heads:
            on = acc_sc[h] * (1.0 / l_sc[h])
            ot = on[:, :tq] - lam * on[:, tq:]
            ms = jnp.mean(ot * ot, axis=0, keepdims=True)
            outs.append((ot * lax.rsqrt(ms + EPS) * g_ref[...] * (1.0 - lam0)).T)
        o_ref[...] = jnp.concatenate(outs, axis=1).astype(o_ref.dtype)


def _attn_kernel(pt_ref, slopes_ref, work_ref,
                 q_ref, kn_ref, vn_ref, lamp_ref, g_ref, kc_hbm, vc_hbm,
                 qt_ref, kp_ref, vt_ref, gcol_ref,
                 xl_ref, gl_ref, cw_ref, cb_ref, wg_ref, bg_ref, lru_lam_ref,
                 o_ref, op_ref, y_ref, conv_ref, hlast_ref,
                 kbuf, vbuf, sem, m_sc, l_sc, acc_sc,
                 pm_sc, pl_sc, pacc_sc, ps_sc, qst_sc,
                 xbuf, a_sc, u_sc, hcar,
                 *, n_seq, nj, n_pg, pg_rows, past, lam0, pool_off, tq, n_lru_t):
    seq = pl.program_id(0)
    j = pl.program_id(1)
    n_chunks = n_seq * nj
    n_slots = kbuf.shape[0]
    g = seq * nj + j
    t_new, da_w = q_ref.shape
    dv = da_w // N_DA_HEADS
    rows_h = 2 * t_new
    n_rows = N_DA_HEADS * rows_h
    n_keys = n_pg * pg_rows // N_DA_HEADS

    def chunk_copies(chunk, slot):
        cs = chunk // nj
        cj = chunk % nj
        copies = []
        for i in range(n_pg):
            page = pt_ref[cs, cj * n_pg + i] + pool_off
            rows = pl.ds(i * pg_rows, pg_rows)
            copies.append(pltpu.make_async_copy(kc_hbm.at[page], kbuf.at[slot, rows], sem.at[0, slot]))
            copies.append(pltpu.make_async_copy(vc_hbm.at[page], vbuf.at[slot, rows], sem.at[1, slot]))
        return copies

    @pl.when(g == 0)
    def _prologue():
        for c in range(min(n_slots - 1, n_chunks)):
            for cp in chunk_copies(c, c):
                cp.start()

    ahead = g + (n_slots - 1)

    @pl.when(ahead < n_chunks)
    def _prefetch():
        for cp in chunk_copies(ahead, ahead % n_slots):
            cp.start()

    work = lambda col: work_ref[g * WORK_COLS + col]

    @pl.when(work(WK_ATT_RUN) == 1)
    def _prompt_attention():
        _prompt_unit(work(WK_ATT_QI), work(WK_ATT_J), slopes_ref,
                     qt_ref, kp_ref, vt_ref, lamp_ref, gcol_ref, op_ref,
                     pm_sc, pl_sc, pacc_sc, ps_sc, qst_sc, tq=tq, lam0=lam0)

    @pl.when(work(WK_LRU_RUN) == 1)
    def _prompt_lru():
        _lru_prompt_unit(work(WK_LRU_T), n_lru_t, xl_ref, gl_ref, cw_ref, cb_ref, wg_ref, bg_ref,
                         lru_lam_ref, y_ref, conv_ref, hlast_ref, xbuf, a_sc, u_sc, hcar)

    slot = g % n_slots
    for cp in chunk_copies(g, slot):
        cp.wait()

    row1 = lax.broadcasted_iota(jnp.int32, (n_rows, 1), 0)
    h_row = row1 // rows_h
    q_row = row1 % t_new
    slope_row = jnp.exp2(-8.0 * (h_row + 1).astype(F32) / N_DA_HEADS)

    qf = q_ref[...].astype(F32)
    lane = lax.broadcasted_iota(jnp.int32, (t_new, dv), 1)
    pieces = []
    for h in range(N_DA_HEADS):
        qh = qf[:, h * dv:(h + 1) * dv]
        pieces.append(jnp.where(lane < dv // 2, qh, 0.0))
        pieces.append(jnp.where(lane >= dv // 2, qh, 0.0))
    qall = jnp.concatenate(pieces, axis=0).astype(BF16)

    @pl.when(j == 0)
    def _init():
        m_sc[...] = jnp.full(m_sc.shape, NEG_BIG, F32)
        l_sc[...] = jnp.zeros(l_sc.shape, F32)
        acc_sc[...] = jnp.zeros(acc_sc.shape, F32)

    grp = MXU_DIM // dv
    n_grp = N_DA_HEADS // grp

    def group_rows(buf, gi):
        return jnp.concatenate([buf[slot, pl.ds(gi * grp + hh, n_keys, stride=N_DA_HEADS), :]
                                for hh in range(grp)], axis=1).astype(BF16)

    zq = jnp.zeros((rows_h, dv), F32)
    s_parts = []
    for gi in range(n_grp):
        qg = jnp.concatenate(
            [jnp.concatenate([jnp.concatenate(pieces[2 * (gi * grp + hh):2 * (gi * grp + hh) + 2], axis=0)
                              if cc == hh else zq for cc in range(grp)], axis=1)
             for hh in range(grp)], axis=0).astype(BF16)
        s_parts.append(_nt_dot(qg, group_rows(kbuf, gi)))
    s = jnp.concatenate(s_parts, axis=0)
    kpos = lax.broadcasted_iota(jnp.int32, (1, n_keys), 1) + (j * n_keys - past)
    s = s + slope_row * kpos.astype(F32)

    def pv(p):
        outs = []
        for gi in range(n_grp):
            og = jnp.dot(p[gi * grp * rows_h:(gi + 1) * grp * rows_h], group_rows(vbuf, gi),
                         preferred_element_type=F32)
            outs += [og[hh * rows_h:(hh + 1) * rows_h, hh * dv:(hh + 1) * dv] for hh in range(grp)]
        return jnp.concatenate(outs, axis=0)

    _softmax_update(s, m_sc, l_sc, acc_sc, pv)

    @pl.when(j == nj - 1)
    def _finish():
        n_new = kn_ref.shape[0]
        pad = jnp.zeros((LANES - n_new, dv), F32)
        kn = jnp.concatenate([kn_ref[...], pad], axis=0).astype(BF16)
        vn = jnp.concatenate([vn_ref[...], pad], axis=0).astype(BF16)
        c = lax.broadcasted_iota(jnp.int32, (n_rows, LANES), 1)
        key = c // N_DA_HEADS
        ok = (c % N_DA_HEADS == h_row) & (key <= q_row) & (c < n_new)
        sn = jnp.where(ok, _nt_dot(qall, kn) + slope_row * key.astype(F32), NEG_BIG)
        _softmax_update(sn, m_sc, l_sc, acc_sc,
                        lambda p: jnp.dot(p, vn, preferred_element_type=F32))

        lam = _lam_value(lamp_ref, lam0)
        on = acc_sc[...] / l_sc[...]
        outs = []
        for h in range(N_DA_HEADS):
            o1 = on[h * rows_h:h * rows_h + t_new]
            o2 = on[h * rows_h + t_new:(h + 1) * rows_h]
            outs.append(_rms(o1 - lam * o2, g_ref[...]) * (1.0 - lam0))
        o_ref[...] = jnp.concatenate(outs, axis=1).astype(o_ref.dtype)


def _attention(q, k_new, v_new, cache_k2, cache_v2, page_table, pool_off, qt, kb, vt, lamp, g_subln,
               slopes, lam0, xl, gl, conv_w, conv_b, wg_bf, bg, lru_lambda, n_pg, tq, tm_lru):
    b, t_new, da_w = q.shape
    bp, s, _ = kb.shape
    lw = xl.shape[-1]
    dv = da_w // N_DA_HEADS
    hist = CONV_WIDTH - 1
    n_pages = page_table.shape[1]
    pg_rows = cache_k2.shape[1]
    past = n_pages * (pg_rows // N_DA_HEADS)
    n_rows = N_DA_HEADS * 2 * t_new
    nj = n_pages // n_pg
    n_lru_t = s // tm_lru
    work = jnp.asarray(_prompt_work_table(b * nj, bp, s // tq, n_lru_t))
    kernel = functools.partial(_attn_kernel, n_seq=b, nj=nj, n_pg=n_pg, pg_rows=pg_rows, past=past,
                               lam0=lam0, pool_off=pool_off, tq=tq, n_lru_t=n_lru_t)

    def wk(col):
        return lambda i, j, pt, sl, w: w[(i * nj + j) * WORK_COLS + col]

    b_in, qi_in, b_out, qi_out = wk(WK_ATT_B), wk(WK_ATT_QI), wk(WK_ATT_OUT_B), wk(WK_ATT_OUT_QI)
    lru_in, lru_out, lru_out_b = wk(WK_LRU_TILE), wk(WK_LRU_OUT_TILE), wk(WK_LRU_OUT_B)
    seq_blk = lambda i, j, pt, sl, w: (i, 0, 0)
    const2 = lambda i, j, pt, sl, w: (0, 0)
    in_specs = [pl.BlockSpec((None, t_new, da_w), seq_blk),
                pl.BlockSpec((None, t_new * N_DA_HEADS, dv), seq_blk),
                pl.BlockSpec((None, t_new * N_DA_HEADS, dv), seq_blk),
                pl.BlockSpec(lamp.shape, const2),
                pl.BlockSpec((1, dv), const2),
                pl.BlockSpec(memory_space=pl.ANY),
                pl.BlockSpec(memory_space=pl.ANY),
                pl.BlockSpec((None, da_w, tq), lambda *a: (b_in(*a), 0, qi_in(*a))),
                pl.BlockSpec((None, s, da_w), lambda *a: (b_in(*a), 0, 0)),
                pl.BlockSpec((None, s // tq, da_w, tq), lambda *a: (b_in(*a), 0, 0, 0)),
                pl.BlockSpec((dv, 1), const2),
                pl.BlockSpec((tm_lru, lw), lambda *a: (lru_in(*a), 0)),
                pl.BlockSpec((tm_lru, lw), lambda *a: (lru_in(*a), 0)),
                pl.BlockSpec((CONV_WIDTH, lw), const2),
                pl.BlockSpec((1, lw), const2),
                pl.BlockSpec(wg_bf.shape, const2),
                pl.BlockSpec((1, 2 * lw), const2),
                pl.BlockSpec((1, lw), const2)]
    out_specs = [pl.BlockSpec((None, t_new, da_w), seq_blk),
                 pl.BlockSpec((None, tq, da_w), lambda *a: (b_out(*a), qi_out(*a), 0)),
                 pl.BlockSpec((tm_lru, lw), lambda *a: (lru_out(*a), 0)),
                 pl.BlockSpec((None, hist, lw), lambda *a: (lru_out_b(*a), 0, 0)),
                 pl.BlockSpec((None, 1, lw), lambda *a: (lru_out_b(*a), 0, 0))]
    ring = pltpu.VMEM((SAMPLE_RING_SLOTS, n_pg * pg_rows, dv), F32)
    return pl.pallas_call(
        kernel,
        grid_spec=pltpu.PrefetchScalarGridSpec(
            num_scalar_prefetch=3,
            grid=(b, nj),
            in_specs=in_specs,
            out_specs=out_specs,
            scratch_shapes=[ring, ring, pltpu.SemaphoreType.DMA((2, SAMPLE_RING_SLOTS)),
                            pltpu.VMEM((n_rows, 1), F32), pltpu.VMEM((n_rows, 1), F32),
                            pltpu.VMEM((n_rows, dv), F32),
                            pltpu.VMEM((N_DA_HEADS, 1, 2 * tq), F32),
                            pltpu.VMEM((N_DA_HEADS, 1, 2 * tq), F32),
                            pltpu.VMEM((N_DA_HEADS, dv, 2 * tq), F32),
                            pltpu.VMEM((2, N_DA_HEADS, tq, 2 * tq), F32),
                            pltpu.VMEM((N_DA_HEADS, dv, 2 * tq), BF16)]
                           + _lru_prompt_scratch(tm_lru, lw)),
        out_shape=[jax.ShapeDtypeStruct((b, t_new, da_w), BF16),
                   jax.ShapeDtypeStruct((bp, s, da_w), BF16),
                   jax.ShapeDtypeStruct((bp * s, lw), BF16),
                   jax.ShapeDtypeStruct((bp, hist, lw), F32),
                   jax.ShapeDtypeStruct((bp, 1, lw), F32)],
        compiler_params=_params("arbitrary", "arbitrary"),
        name="attention",
    )(page_table, slopes, work, q, k_new, v_new, lamp, g_subln.reshape(1, dv), cache_k2, cache_v2,
      qt, kb, vt, g_subln.reshape(dv, 1),
      xl, gl, conv_w, conv_b.reshape(1, lw), wg_bf, bg.reshape(1, 2 * lw), lru_lambda.reshape(1, lw))


def _softplus(z):
    return jnp.maximum(z, 0.0) + jnp.log1p(jnp.exp(-jnp.abs(z)))


def _gelu_tanh(x):
    return 0.5 * x * (1.0 + jnp.tanh(math.sqrt(2.0 / math.pi) * (x + 0.044715 * (x * x * x))))


def _lru_gates(xc, wg_ref, bg_ref, lam_ref):
    w = xc.shape[-1]
    g = jnp.dot(xc.astype(BF16), wg_ref[...], preferred_element_type=F32) + bg_ref[...]
    r = jax.nn.sigmoid(g[:, :w])
    ig = jax.nn.sigmoid(g[:, w:])
    log_a = -LRU_C * r * _softplus(-lam_ref[...])
    a = jnp.exp(log_a)
    u = jnp.sqrt(-jnp.tanh(log_a) * (1.0 + a * a)) * (ig * xc)
    return a, u


def _load_blocked(sc, rows):
    return jnp.concatenate([sc[c, rows, :] for c in range(sc.shape[0])], axis=1)


def _store_blocked(sc, rows, val):
    for c in range(sc.shape[0]):
        sc[c, rows, :] = val[:, c * LANES:(c + 1) * LANES]


def _blocked(rows, w):
    return pltpu.VMEM((w // LANES, rows, LANES), F32)


def _lru_prompt_unit(t, nt, xl_ref, gl_ref, cw_ref, cb_ref, wg_ref, bg_ref, lam_ref,
                     y_ref, conv_ref, hlast_ref, xbuf, a_sc, u_sc, hcar):
    tm, w = xl_ref.shape
    seg = tm // LRU_SEGMENTS
    pitch = seg + LRU_SEG_PAD
    hist = CONV_WIDTH - 1

    @pl.when(t == 0)
    def _init():
        xbuf[0:SUBLANES, :] = jnp.zeros((SUBLANES, w), F32)
        hcar[...] = jnp.zeros(hcar.shape, F32)

    xbuf[SUBLANES:SUBLANES + tm, :] = xl_ref[...]
    xc = cb_ref[...]
    for jj in range(CONV_WIDTH):
        xc = xc + xbuf[pl.ds(SUBLANES - hist + jj, tm), :] * cw_ref[jj:jj + 1, :]
    a, u = _lru_gates(xc, wg_ref, bg_ref, lam_ref)
    for s in range(LRU_SEGMENTS):
        _store_blocked(a_sc, slice(s * pitch, s * pitch + seg), a[s * seg:(s + 1) * seg])
        _store_blocked(u_sc, slice(s * pitch, s * pitch + seg), u[s * seg:(s + 1) * seg])

    def step(i, carry):
        p, hh = carry
        rows = pl.ds(i, LRU_SEGMENTS, stride=pitch)
        ai = _load_blocked(a_sc, rows)
        p = ai * p
        hh = ai * hh + _load_blocked(u_sc, rows)
        _store_blocked(a_sc, rows, p)
        _store_blocked(u_sc, rows, hh)
        return p, hh

    p_end, h_end = lax.fori_loop(0, seg, step,
                                 (jnp.ones((LRU_SEGMENTS, w), F32), jnp.zeros((LRU_SEGMENTS, w), F32)))

    h_in = hcar[...]
    for s in range(LRU_SEGMENTS):
        blk = slice(s * seg, (s + 1) * seg)
        sblk = slice(s * pitch, s * pitch + seg)
        hs = _load_blocked(u_sc, sblk) + _load_blocked(a_sc, sblk) * h_in
        y_ref[blk, :] = (hs * _gelu_tanh(gl_ref[blk, :])).astype(y_ref.dtype)
        h_in = p_end[s:s + 1, :] * h_in + h_end[s:s + 1, :]
    hcar[...] = h_in
    xbuf[0:SUBLANES, :] = xbuf[tm:tm + SUBLANES, :]

    @pl.when(t == nt - 1)
    def _fin():
        hlast_ref[...] = h_in
        conv_ref[...] = xbuf[pl.ds(SUBLANES - hist, hist), :]


def _lru_prompt_scratch(tm, w):
    scan_rows = LRU_SEGMENTS * (tm // LRU_SEGMENTS + LRU_SEG_PAD)
    return [pltpu.VMEM((tm + SUBLANES, w), F32), _blocked(scan_rows, w), _blocked(scan_rows, w),
            pltpu.VMEM((1, w), F32)]


def _lru_sample_kernel(xl_ref, gl_ref, cbuf_ref, h0_ref, cw_ref, cb_ref, wg_ref, bg_ref, lam_ref,
                       y_ref, conv_ref, hlast_ref, x_sc, g_sc, c_sc, y_sc, *, t_new):
    nb = h0_ref.shape[0]
    hist = CONV_WIDTH - 1
    _store_blocked(x_sc, slice(None), xl_ref[...])
    _store_blocked(g_sc, slice(None), gl_ref[...])
    _store_blocked(c_sc, slice(None), cbuf_ref[...])
    xp = [_load_blocked(c_sc, pl.ds(jj, nb, stride=hist)) for jj in range(hist)]
    xp += [_load_blocked(x_sc, pl.ds(tt, nb, stride=t_new)) for tt in range(t_new)]
    hh = h0_ref[...]
    for tt in range(t_new):
        xc = cb_ref[...]
        for jj in range(CONV_WIDTH):
            xc = xc + xp[tt + jj] * cw_ref[jj:jj + 1, :]
        a, u = _lru_gates(xc, wg_ref, bg_ref, lam_ref)
        hh = a * hh + u
        gate = _gelu_tanh(_load_blocked(g_sc, pl.ds(tt, nb, stride=t_new)))
        _store_blocked(y_sc, pl.ds(tt, nb, stride=t_new), hh * gate)
    hlast_ref[...] = hh
    y_ref[...] = _load_blocked(y_sc, slice(None)).astype(y_ref.dtype)
    for jj in range(hist):
        _store_blocked(c_sc, pl.ds(jj, nb, stride=hist), xp[t_new + jj])
    conv_ref[...] = _load_blocked(c_sc, slice(None))


def _lru_sample(xl, gl, conv_buf, h0, conv_w, conv_b, wg_bf, bg, lru_lambda, t_new):
    n_tok, w = xl.shape
    nb = n_tok // t_new
    hist = CONV_WIDTH - 1
    full = lambda shape: pl.BlockSpec(shape, lambda i: (0,) * len(shape))
    return pl.pallas_call(
        functools.partial(_lru_sample_kernel, t_new=t_new),
        grid=(1,),
        in_specs=[full((n_tok, w)), full((n_tok, w)), full((nb * hist, w)), full((nb, w)),
                  full((CONV_WIDTH, w)), full((1, w)), full(wg_bf.shape), full((1, 2 * w)),
                  full((1, w))],
        out_specs=[full((n_tok, w)), full((nb * hist, w)), full((nb, w))],
        out_shape=[jax.ShapeDtypeStruct((n_tok, w), BF16),
                   jax.ShapeDtypeStruct((nb * hist, w), F32),
                   jax.ShapeDtypeStruct((nb, w), F32)],
        scratch_shapes=[_blocked(n_tok, w), _blocked(n_tok, w), _blocked(nb * hist, w),
                        _blocked(n_tok, w)],
        compiler_params=_params("arbitrary"),
        name="lru_sample",
    )(xl, gl, conv_buf.reshape(nb * hist, w), h0, conv_w, conv_b.reshape(1, w), wg_bf,
      bg.reshape(1, 2 * w), lru_lambda.reshape(1, w))


def _mlp_kernel(x_ref, o_ref, y_ref, gt1_ref, sc2_ref, sh2_ref, gt2_ref, g2_ref, gf_ref,
                wout_ref, w1_ref, w2_ref, out_ref, *, ff_chunk, final_norm):
    bb, tt, d = x_ref.shape
    rows = bb * tt
    mix_in = jnp.concatenate([o_ref[...], y_ref[...]], axis=1)
    mix = jnp.dot(mix_in, wout_ref[...], preferred_element_type=F32).reshape(bb, tt, d)
    x1 = x_ref[...] + gt1_ref[...] * mix
    h2 = (_rms(x1, g2_ref[...]) * (1.0 + sc2_ref[...]) + sh2_ref[...]).reshape(rows, d).astype(BF16)
    d_ff = w1_ref.shape[1]
    ff = jnp.zeros((rows, d), F32)
    for c in range(d_ff // ff_chunk):
        cs = slice(c * ff_chunk, (c + 1) * ff_chunk)
        hc = jnp.dot(h2, w1_ref[:, cs], preferred_element_type=F32)
        hc = jnp.square(jnp.maximum(hc, 0.0)).astype(BF16)
        ff = ff + jnp.dot(hc, w2_ref[cs, :], preferred_element_type=F32)
    x2 = x1 + gt2_ref[...] * ff.reshape(bb, tt, d)
    out_ref[...] = _rms(x2, gf_ref[...]) if final_norm else x2


def _mlp(x, o, y, gt1, sc2, sh2, gt2, g2, gf, wout_bf, w1_bf, w2_bf, bb, tt, final_norm):
    b, t, d = x.shape
    nt = t // tt
    rows = bb * tt
    mix_w = o.shape[-1]
    mod = pl.BlockSpec((bb, 1, d), lambda i, j: (i, 0, 0))
    gain = pl.BlockSpec((1, 1, d), lambda i, j: (0, 0, 0))
    wspec = lambda w: pl.BlockSpec(w.shape, lambda i, j: (0, 0), pipeline_mode=pl.Buffered(1))
    return pl.pallas_call(
        functools.partial(_mlp_kernel, ff_chunk=min(1024, w1_bf.shape[1]), final_norm=final_norm),
        grid=(b // bb, nt),
        in_specs=[pl.BlockSpec((bb, tt, d), lambda i, j: (i, j, 0)),
                  pl.BlockSpec((rows, mix_w), lambda i, j: (i * nt + j, 0)),
                  pl.BlockSpec((rows, y.shape[-1]), lambda i, j: (i * nt + j, 0)),
                  mod, mod, mod, mod, gain, gain, wspec(wout_bf), wspec(w1_bf), wspec(w2_bf)],
        out_specs=pl.BlockSpec((bb, tt, d), lambda i, j: (i, j, 0)),
        out_shape=jax.ShapeDtypeStruct((b, t, d), F32),
        compiler_params=_params("arbitrary", "arbitrary"),
        name="out_mlp",
    )(x, o, y, gt1, sc2, sh2, gt2, g2.reshape(1, 1, d), gf.reshape(1, 1, d), wout_bf, w1_bf, w2_bf)


def _block_diag(w):
    h, i, j = w.shape
    eye = jnp.eye(h, dtype=w.dtype)
    return (eye[:, None, :, None] * w[:, :, None, :]).reshape(h * i, h * j)


def _pick(n, target):
    t = min(n, target)
    while n % t:
        t -= 1
    return t


def kernel(x_prompt, x_sample, c_prompt, c_sample, cache_k, cache_v, page_table, state_h, state_conv, w_ada, b_ada, g_norm1, g_norm2, w_in, lambda_q1, lambda_k1, lambda_q2, lambda_k2, g_subln, conv_w, conv_b, w_rg, b_rg, w_ig, b_ig, lru_lambda, w_out, w_ff1, w_ff2, g_final):
    depth = w_in.shape[0]
    bp, seq, d = x_prompt.shape
    bs, t_new, _ = x_sample.shape
    lru_w = conv_w.shape[-1]
    da_w = (w_in.shape[-1] - 2 * lru_w) // 3
    dv = da_w // N_DA_HEADS
    n_pool, page_size = cache_k.shape[1], cache_k.shape[2]
    n_pages = page_table.shape[1]
    hist = CONV_WIDTH - 1

    slopes = 2.0 ** (-8.0 * jnp.arange(1, N_DA_HEADS + 1, dtype=F32) / N_DA_HEADS)
    cache_k2 = cache_k.reshape(depth * n_pool, page_size * N_DA_HEADS, dv)
    cache_v2 = cache_v.reshape(depth * n_pool, page_size * N_DA_HEADS, dv)

    tt_p = _pick(seq, 512)
    bb_s = _pick(bs, 64)
    tq = _pick(seq, 256)
    n_pg = _pick(n_pages, 16)
    tm_lru = _pick(seq, 128)

    yp, ys = x_prompt, x_sample
    kp_l, vp_l, cp_l, hp_l, ks_l, vs_l, cs_l, hs_l = [], [], [], [], [], [], [], []
    for l in range(depth):
        lam0 = _lambda_init(l)
        w_in_bf = w_in[l].astype(BF16)
        w_out_bf = w_out[l].astype(BF16)
        w1_bf = w_ff1[l].astype(BF16)
        w2_bf = w_ff2[l].astype(BF16)
        wg_bf = jnp.concatenate([_block_diag(w_rg[l]), _block_diag(w_ig[l])], axis=1).astype(BF16)
        bg = jnp.concatenate([b_rg[l], b_ig[l]])
        lamp = jnp.stack([lambda_q1[l], lambda_k1[l], lambda_q2[l], lambda_k2[l]])
        final = l == depth - 1

        mod = _ada_mod(jnp.concatenate([c_prompt, c_sample], axis=0), w_ada[l], b_ada[l])
        mods_p = [m[:, None, :] for m in jnp.split(mod[:bp], 6, axis=-1)]
        mods_s = [m[:, None, :] for m in jnp.split(mod[bp:], 6, axis=-1)]

        qt, kp, vp, kb, vt, xl_p, gl_p = _in_proj(yp, mods_p[1], mods_p[0], g_norm1[l], w_in_bf,
                                                  1, tt_p, da_w, lru_w, tk=tq)
        q, ks, vs, xl_s, gl_s = _in_proj(ys, mods_s[1], mods_s[0], g_norm1[l], w_in_bf,
                                         bb_s, t_new, da_w, lru_w)
        o_s, o_p, y_lru, cp, hp = _attention(
            q.reshape(bs, t_new, da_w), ks.reshape(bs, t_new * N_DA_HEADS, dv),
            vs.reshape(bs, t_new * N_DA_HEADS, dv), cache_k2, cache_v2, page_table, l * n_pool,
            qt, kb.reshape(bp, seq, da_w), vt, lamp, g_subln[l], slopes, lam0,
            xl_p, gl_p, conv_w[l], conv_b[l], wg_bf, bg, lru_lambda[l], n_pg, tq, tm_lru)

        yp = _mlp(yp, o_p.reshape(bp * seq, da_w), y_lru, mods_p[2], mods_p[4], mods_p[3],
                  mods_p[5], g_norm2[l], g_final, w_out_bf, w1_bf, w2_bf, 1, tt_p, final)
        kp_l.append(kp.reshape(bp, seq, N_DA_HEADS, dv))
        vp_l.append(vp.reshape(bp, seq, N_DA_HEADS, dv))
        cp_l.append(cp)
        hp_l.append(hp.reshape(bp, lru_w))

        y_lru, cs, hs = _lru_sample(xl_s, gl_s, state_conv[l], state_h[l], conv_w[l], conv_b[l],
                                    wg_bf, bg, lru_lambda[l], t_new)
        ys = _mlp(ys, o_s.reshape(bs * t_new, da_w), y_lru, mods_s[2], mods_s[4], mods_s[3],
                  mods_s[5], g_norm2[l], g_final, w_out_bf, w1_bf, w2_bf, bb_s, t_new, final)
        ks_l.append(ks.reshape(bs, t_new, N_DA_HEADS, dv))
        vs_l.append(vs.reshape(bs, t_new, N_DA_HEADS, dv))
        cs_l.append(cs.reshape(bs, hist, lru_w))
        hs_l.append(hs)

    return (yp, ys,
            jnp.stack(kp_l), jnp.stack(vp_l), jnp.stack(cp_l), jnp.stack(hp_l),
            jnp.stack(ks_l), jnp.stack(vs_l), jnp.stack(cs_l), jnp.stack(hs_l))
```

```python
import functools
import math

import jax
import jax.numpy as jnp
import numpy as np
from jax import lax
from jax.experimental import pallas as pl
from jax.experimental.pallas import tpu as pltpu

F32 = jnp.float32
BF16 = jnp.bfloat16

N_DA_HEADS = 4
N_LRU_HEADS = 8
CONV_WIDTH = 4
LRU_C = 8.0
EPS = 1e-6
NEG_BIG = -1e30
SUBLANES = 8
LANES = 128
MXU_DIM = 256
VMEM_LIMIT_BYTES = 56 * 1024 * 1024
SAMPLE_RING_SLOTS = 3
LRU_SEGMENTS = SUBLANES
LRU_SEG_PAD = SUBLANES


def _lambda_init(layer):
    return 0.8 - 0.6 * math.exp(-0.3 * layer)


def _params(*sem):
    return pltpu.CompilerParams(dimension_semantics=sem, vmem_limit_bytes=VMEM_LIMIT_BYTES)


def _rms(x, g):
    return x * lax.rsqrt(jnp.mean(x * x, axis=-1, keepdims=True) + EPS) * g


def _ada_kernel(c_ref, w_ref, b_ref, o_ref):
    c = c_ref[...]
    s = (c * jax.nn.sigmoid(c)).astype(BF16)
    o_ref[...] = jnp.dot(s, w_ref[...].astype(BF16), preferred_element_type=F32) + b_ref[...]


def _ada_mod(c, w, b, tn=1536):
    m, d = c.shape
    n = w.shape[1]
    return pl.pallas_call(
        _ada_kernel,
        grid=(n // tn,),
        in_specs=[pl.BlockSpec((m, d), lambda j: (0, 0)),
                  pl.BlockSpec((d, tn), lambda j: (0, j)),
                  pl.BlockSpec((1, tn), lambda j: (0, j))],
        out_specs=pl.BlockSpec((m, tn), lambda j: (0, j)),
        out_shape=jax.ShapeDtypeStruct((m, n), F32),
        compiler_params=_params("arbitrary"),
        name="ada_mod",
    )(c, w, b.reshape(1, n))


def _inproj_kernel(x_ref, sc_ref, sh_ref, g_ref, w_ref, *refs, da_w, lru_w, q_scale, tk):
    bb, tt, d = x_ref.shape
    rows = bb * tt
    dv = da_w // N_DA_HEADS
    h = _rms(x_ref[...], g_ref[...]) * (1.0 + sc_ref[...]) + sh_ref[...]
    h = h.reshape(rows, d).astype(BF16)
    proj = jnp.dot(h, w_ref[...], preferred_element_type=F32)
    q = proj[:, :da_w] * q_scale
    k = proj[:, da_w:2 * da_w]
    v = proj[:, 2 * da_w:3 * da_w]
    if tk is None:
        q_ref, k4_ref, v4_ref, xl_ref, gl_ref = refs
        q_ref[...] = q.astype(BF16)
    else:
        qt_ref, k4_ref, v4_ref, kb_ref, vt_ref, xl_ref, gl_ref = refs
        qt_ref[...] = q.T.astype(BF16)
        kb_ref[...] = k.astype(BF16)
        vt = v.T.astype(BF16)
        for c in range(rows // tk):
            vt_ref[c] = vt[:, c * tk:(c + 1) * tk]
    for hh in range(N_DA_HEADS):
        k4_ref[pl.ds(hh, rows, stride=N_DA_HEADS), :] = k[:, hh * dv:(hh + 1) * dv]
        v4_ref[pl.ds(hh, rows, stride=N_DA_HEADS), :] = v[:, hh * dv:(hh + 1) * dv]
    xl_ref[...] = proj[:, 3 * da_w:3 * da_w + lru_w]
    gl_ref[...] = proj[:, 3 * da_w + lru_w:]


def _in_proj(x, sc, sh, g, w_bf, bb, tt, da_w, lru_w, tk=None):
    b, t, d = x.shape
    n_tok = b * t
    nt = t // tt
    rows = bb * tt
    dv = da_w // N_DA_HEADS
    dqk = dv // 2

    def flat(width, dtype, mult=1):
        return (pl.BlockSpec((rows * mult, width), lambda i, j: (i * nt + j, 0)),
                jax.ShapeDtypeStruct((n_tok * mult, width), dtype))

    kv4 = [flat(dv, F32, N_DA_HEADS), flat(dv, F32, N_DA_HEADS)]
    lru = [flat(lru_w, F32), flat(lru_w, F32)]
    if tk is None:
        outs = [flat(da_w, BF16)] + kv4 + lru
    else:
        assert bb == 1 and tt % tk == 0
        qt = (pl.BlockSpec((None, da_w, tt), lambda i, j: (i, 0, j)),
              jax.ShapeDtypeStruct((b, da_w, t), BF16))
        vt = (pl.BlockSpec((None, tt // tk, da_w, tk), lambda i, j: (i, j, 0, 0)),
              jax.ShapeDtypeStruct((b, t // tk, da_w, tk), BF16))
        outs = [qt] + kv4 + [flat(da_w, BF16), vt] + lru
    return pl.pallas_call(
        functools.partial(_inproj_kernel, da_w=da_w, lru_w=lru_w, q_scale=dqk ** -0.5, tk=tk),
        grid=(b // bb, nt),
        in_specs=[pl.BlockSpec((bb, tt, d), lambda i, j: (i, j, 0)),
                  pl.BlockSpec((bb, 1, d), lambda i, j: (i, 0, 0)),
                  pl.BlockSpec((bb, 1, d), lambda i, j: (i, 0, 0)),
                  pl.BlockSpec((1, 1, d), lambda i, j: (0, 0, 0)),
                  pl.BlockSpec(w_bf.shape, lambda i, j: (0, 0))],
        out_specs=[o[0] for o in outs],
        out_shape=[o[1] for o in outs],
        compiler_params=_params("arbitrary", "arbitrary"),
        name="in_proj",
    )(x, sc, sh, g.reshape(1, 1, d), w_bf)


def _lam_value(lamp_ref, lam0):
    lp = lamp_ref[...]
    t1 = jnp.sum(lp[0:1] * lp[1:2], axis=-1, keepdims=True)
    t2 = jnp.sum(lp[2:3] * lp[3:4], axis=-1, keepdims=True)
    return jnp.exp(t1) - jnp.exp(t2) + lam0


def _softmax_update(s, m_sc, l_sc, acc_sc, pv_fn):
    m_old = m_sc[...]
    m_new = jnp.maximum(m_old, jnp.max(s, axis=-1, keepdims=True))
    alpha = jnp.exp(m_old - m_new)
    p = jnp.exp(s - m_new)
    l_sc[...] = alpha * l_sc[...] + jnp.sum(p, axis=-1, keepdims=True)
    acc_sc[...] = alpha * acc_sc[...] + pv_fn(p.astype(BF16))
    m_sc[...] = m_new


def _nt_dot(a, b):
    return lax.dot_general(a, b, (((1,), (1,)), ((), ())), preferred_element_type=F32)


(WK_ATT_RUN, WK_ATT_B, WK_ATT_QI, WK_ATT_J, WK_ATT_OUT_B, WK_ATT_OUT_QI, WORK_COLS) = range(7)


def _prompt_work_table(n_steps, n_batch, n_qt):
    units = [(b, qi, j) for b in range(n_batch) for qi in range(n_qt) for j in range(qi + 1)]
    assert len(units) <= n_steps, "prompt attention units must fit the sample-attention grid"
    step_of = {u * n_steps // len(units): u for u in range(len(units))}
    table = np.zeros((n_steps, WORK_COLS), np.int32)
    nxt, done = 0, 0
    for g in range(n_steps):
        u = step_of.get(g)
        if u is not None:
            nxt, done = u + 1, u
        b_in, qi_in, j_in = units[u if u is not None else min(nxt, len(units) - 1)]
        table[g] = (u is not None, b_in, qi_in, j_in) + units[done][:2]
    return table.reshape(-1)


def _prompt_unit(qi, j, slopes_ref, qt_ref, k_ref, vt_ref, lamp_ref, g_ref, o_ref,
                 m_sc, l_sc, acc_sc, s_sc, qst_sc, *, tq, lam0):
    dv = g_ref.shape[0]
    heads = range(N_DA_HEADS)
    cols = [slice(h * dv, (h + 1) * dv) for h in heads]
    key = lax.broadcasted_iota(jnp.int32, (tq, LANES), 0).astype(F32)

    def scores(jj, slot):
        start = pl.multiple_of(jj * tq, tq)
        for h in heads:
            s_sc[slot, h] = jnp.dot(k_ref[pl.ds(start, tq), cols[h]], qst_sc[h],
                                    preferred_element_type=F32)

    def update(slot, masked):
        if masked:
            kk = lax.broadcasted_iota(jnp.int32, (tq, 2 * tq), 0)
            qq = lax.broadcasted_iota(jnp.int32, (tq, 2 * tq), 1)
            visible = kk <= jnp.where(qq >= tq, qq - tq, qq)
        tile_off = ((j - qi) * tq).astype(F32)
        for h in heads:
            s = s_sc[slot, h] + jnp.concatenate([slopes_ref[h] * key] * (2 * tq // LANES), axis=1)
            if masked:
                s = jnp.where(visible, s, NEG_BIG)
            shift = slopes_ref[h] * tile_off
            m_old = m_sc[h]
            m_new = jnp.maximum(m_old, jnp.max(s, axis=0, keepdims=True) + shift)
            alpha = jnp.exp(m_old - m_new)
            p = jnp.exp(s - (m_new - shift))
            l_sc[h] = alpha * l_sc[h] + jnp.sum(p, axis=0, keepdims=True)
            acc_sc[h] = alpha * acc_sc[h] + jnp.dot(vt_ref[j, cols[h], :], p.astype(BF16),
                                                    preferred_element_type=F32)
            m_sc[h] = m_new

    @pl.when(j == 0)
    def _start():
        row = lax.broadcasted_iota(jnp.int32, (dv, tq), 0)
        zero = jnp.zeros((dv, tq), BF16)
        for h in heads:
            qt = qt_ref[cols[h], :]
            qst_sc[h] = jnp.concatenate([jnp.where(row < dv // 2, qt, zero),
                                         jnp.where(row >= dv // 2, qt, zero)], axis=1)
        m_sc[...] = jnp.full(m_sc.shape, NEG_BIG, F32)
        l_sc[...] = jnp.zeros(l_sc.shape, F32)
        acc_sc[...] = jnp.zeros(acc_sc.shape, F32)
        scores(0, 0)

    for parity in (0, 1):
        @pl.when((j < qi) & (j % 2 == parity))
        def _off_diagonal():
            scores(j + 1, 1 - parity)
            update(parity, False)

        @pl.when((j == qi) & (j % 2 == parity))
        def _diagonal():
            update(parity, True)

    @pl.when(j == qi)
    def _finish():
        lam = _lam_value(lamp_ref, lam0)
        outs = []
        for h in heads:
            on = acc_sc[h] * (1.0 / l_sc[h])
            ot = on[:, :tq] - lam * on[:, tq:]
            ms = jnp.mean(ot * ot, axis=0, keepdims=True)
            outs.append((ot * lax.rsqrt(ms + EPS) * g_ref[...] * (1.0 - lam0)).T)
        o_ref[...] = jnp.concatenate(outs, axis=1).astype(o_ref.dtype)


def _attn_kernel(pt_ref, slopes_ref, work_ref,
                 q_ref, kn_ref, vn_ref, lamp_ref, g_ref, kc_hbm, vc_hbm,
                 qt_ref, kp_ref, vt_ref, gcol_ref,
                 o_ref, op_ref,
                 kbuf, vbuf, sem, m_sc, l_sc, acc_sc,
                 pm_sc, pl_sc, pacc_sc, ps_sc, qst_sc,
                 *, n_seq, nj, n_pg, pg_rows, past, lam0, pool_off, tq):
    seq = pl.program_id(0)
    j = pl.program_id(1)
    n_chunks = n_seq * nj
    n_slots = kbuf.shape[0]
    g = seq * nj + j
    t_new, da_w = q_ref.shape
    dv = da_w // N_DA_HEADS
    rows_h = 2 * t_new
    n_rows = N_DA_HEADS * rows_h
    n_keys = n_pg * pg_rows // N_DA_HEADS

    def chunk_copies(chunk, slot):
        cs = chunk // nj
        cj = chunk % nj
        copies = []
        for i in range(n_pg):
            page = pt_ref[cs, cj * n_pg + i] + pool_off
            rows = pl.ds(i * pg_rows, pg_rows)
            copies.append(pltpu.make_async_copy(kc_hbm.at[page], kbuf.at[slot, rows], sem.at[0, slot]))
            copies.append(pltpu.make_async_copy(vc_hbm.at[page], vbuf.at[slot, rows], sem.at[1, slot]))
        return copies

    @pl.when(g == 0)
    def _prologue():
        for c in range(min(n_slots - 1, n_chunks)):
            for cp in chunk_copies(c, c):
                cp.start()

    ahead = g + (n_slots - 1)

    @pl.when(ahead < n_chunks)
    def _prefetch():
        for cp in chunk_copies(ahead, ahead % n_slots):
            cp.start()

    work = lambda col: work_ref[g * WORK_COLS + col]

    @pl.when(work(WK_ATT_RUN) == 1)
    def _prompt_attention():
        _prompt_unit(work(WK_ATT_QI), work(WK_ATT_J), slopes_ref,
                     qt_ref, kp_ref, vt_ref, lamp_ref, gcol_ref, op_ref,
                     pm_sc, pl_sc, pacc_sc, ps_sc, qst_sc, tq=tq, lam0=lam0)

    slot = g % n_slots
    for cp in chunk_copies(g, slot):
        cp.wait()

    row1 = lax.broadcasted_iota(jnp.int32, (n_rows, 1), 0)
    h_row = row1 // rows_h
    q_row = row1 % t_new
    slope_row = jnp.exp2(-8.0 * (h_row + 1).astype(F32) / N_DA_HEADS)

    qf = q_ref[...].astype(F32)
    lane = lax.broadcasted_iota(jnp.int32, (t_new, dv), 1)
    pieces = []
    for h in range(N_DA_HEADS):
        qh = qf[:, h * dv:(h + 1) * dv]
        pieces.append(jnp.where(lane < dv // 2, qh, 0.0))
        pieces.append(jnp.where(lane >= dv // 2, qh, 0.0))
    qall = jnp.concatenate(pieces, axis=0).astype(BF16)

    @pl.when(j == 0)
    def _init():
        m_sc[...] = jnp.full(m_sc.shape, NEG_BIG, F32)
        l_sc[...] = jnp.zeros(l_sc.shape, F32)
        acc_sc[...] = jnp.zeros(acc_sc.shape, F32)

    grp = MXU_DIM // dv
    n_grp = N_DA_HEADS // grp

    def group_rows(buf, gi):
        return jnp.concatenate([buf[slot, pl.ds(gi * grp + hh, n_keys, stride=N_DA_HEADS), :]
                                for hh in range(grp)], axis=1).astype(BF16)

    zq = jnp.zeros((rows_h, dv), F32)
    s_parts = []
    for gi in range(n_grp):
        qg = jnp.concatenate(
            [jnp.concatenate([jnp.concatenate(pieces[2 * (gi * grp + hh):2 * (gi * grp + hh) + 2], axis=0)
                              if cc == hh else zq for cc in range(grp)], axis=1)
             for hh in range(grp)], axis=0).astype(BF16)
        s_parts.append(_nt_dot(qg, group_rows(kbuf, gi)))
    s = jnp.concatenate(s_parts, axis=0)
    kpos = lax.broadcasted_iota(jnp.int32, (1, n_keys), 1) + (j * n_keys - past)
    s = s + slope_row * kpos.astype(F32)

    def pv(p):
        outs = []
        for gi in range(n_grp):
            og = jnp.dot(p[gi * grp * rows_h:(gi + 1) * grp * rows_h], group_rows(vbuf, gi),
                         preferred_element_type=F32)
            outs += [og[hh * rows_h:(hh + 1) * rows_h, hh * dv:(hh + 1) * dv] for hh in range(grp)]
        return jnp.concatenate(outs, axis=0)

    _softmax_update(s, m_sc, l_sc, acc_sc, pv)

    @pl.when(j == nj - 1)
    def _finish():
        n_new = kn_ref.shape[0]
        pad = jnp.zeros((LANES - n_new, dv), F32)
        kn = jnp.concatenate([kn_ref[...], pad], axis=0).astype(BF16)
        vn = jnp.concatenate([vn_ref[...], pad], axis=0).astype(BF16)
        c = lax.broadcasted_iota(jnp.int32, (n_rows, LANES), 1)
        key = c // N_DA_HEADS
        ok = (c % N_DA_HEADS == h_row) & (key <= q_row) & (c < n_new)
        sn = jnp.where(ok, _nt_dot(qall, kn) + slope_row * key.astype(F32), NEG_BIG)
        _softmax_update(sn, m_sc, l_sc, acc_sc,
                        lambda p: jnp.dot(p, vn, preferred_element_type=F32))

        lam = _lam_value(lamp_ref, lam0)
        on = acc_sc[...] / l_sc[...]
        outs = []
        for h in range(N_DA_HEADS):
            o1 = on[h * rows_h:h * rows_h + t_new]
            o2 = on[h * rows_h + t_new:(h + 1) * rows_h]
            outs.append(_rms(o1 - lam * o2, g_ref[...]) * (1.0 - lam0))
        o_ref[...] = jnp.concatenate(outs, axis=1).astype(o_ref.dtype)


def _attention(q, k_new, v_new, cache_k2, cache_v2, page_table, pool_off, qt, kb, vt, lamp, g_subln,
               slopes, lam0, n_pg, tq):
    b, t_new, da_w = q.shape
    bp, s, _ = kb.shape
    dv = da_w // N_DA_HEADS
    n_pages = page_table.shape[1]
    pg_rows = cache_k2.shape[1]
    past = n_pages * (pg_rows // N_DA_HEADS)
    n_rows = N_DA_HEADS * 2 * t_new
    nj = n_pages // n_pg
    work = jnp.asarray(_prompt_work_table(b * nj, bp, s // tq))
    kernel = functools.partial(_attn_kernel, n_seq=b, nj=nj, n_pg=n_pg, pg_rows=pg_rows, past=past,
                               lam0=lam0, pool_off=pool_off, tq=tq)

    def wk(col):
        return lambda i, j, pt, sl, w: w[(i * nj + j) * WORK_COLS + col]

    b_in, qi_in, b_out, qi_out = wk(WK_ATT_B), wk(WK_ATT_QI), wk(WK_ATT_OUT_B), wk(WK_ATT_OUT_QI)
    seq_blk = lambda i, j, pt, sl, w: (i, 0, 0)
    const2 = lambda i, j, pt, sl, w: (0, 0)
    in_specs = [pl.BlockSpec((None, t_new, da_w), seq_blk),
                pl.BlockSpec((None, t_new * N_DA_HEADS, dv), seq_blk),
                pl.BlockSpec((None, t_new * N_DA_HEADS, dv), seq_blk),
                pl.BlockSpec(lamp.shape, const2),
                pl.BlockSpec((1, dv), const2),
                pl.BlockSpec(memory_space=pl.ANY),
                pl.BlockSpec(memory_space=pl.ANY),
                pl.BlockSpec((None, da_w, tq), lambda *a: (b_in(*a), 0, qi_in(*a))),
                pl.BlockSpec((None, s, da_w), lambda *a: (b_in(*a), 0, 0)),
                pl.BlockSpec((None, s // tq, da_w, tq), lambda *a: (b_in(*a), 0, 0, 0)),
                pl.BlockSpec((dv, 1), const2)]
    out_specs = [pl.BlockSpec((None, t_new, da_w), seq_blk),
                 pl.BlockSpec((None, tq, da_w), lambda *a: (b_out(*a), qi_out(*a), 0))]
    ring = pltpu.VMEM((SAMPLE_RING_SLOTS, n_pg * pg_rows, dv), F32)
    return pl.pallas_call(
        kernel,
        grid_spec=pltpu.PrefetchScalarGridSpec(
            num_scalar_prefetch=3,
            grid=(b, nj),
            in_specs=in_specs,
            out_specs=out_specs,
            scratch_shapes=[ring, ring, pltpu.SemaphoreType.DMA((2, SAMPLE_RING_SLOTS)),
                            pltpu.VMEM((n_rows, 1), F32), pltpu.VMEM((n_rows, 1), F32),
                            pltpu.VMEM((n_rows, dv), F32),
                            pltpu.VMEM((N_DA_HEADS, 1, 2 * tq), F32),
                            pltpu.VMEM((N_DA_HEADS, 1, 2 * tq), F32),
                            pltpu.VMEM((N_DA_HEADS, dv, 2 * tq), F32),
                            pltpu.VMEM((2, N_DA_HEADS, tq, 2 * tq), F32),
                            pltpu.VMEM((N_DA_HEADS, dv, 2 * tq), BF16)]),
        out_shape=[jax.ShapeDtypeStruct((b, t_new, da_w), BF16),
                   jax.ShapeDtypeStruct((bp, s, da_w), BF16)],
        compiler_params=_params("arbitrary", "arbitrary"),
        name="attention",
    )(page_table, slopes, work, q, k_new, v_new, lamp, g_subln.reshape(1, dv), cache_k2, cache_v2,
      qt, kb, vt, g_subln.reshape(dv, 1))


def _softplus(z):
    return jnp.maximum(z, 0.0) + jnp.log1p(jnp.exp(-jnp.abs(z)))


def _sigmoid(x):
    return 0.5 * jnp.tanh(0.5 * x) + 0.5


def _gelu_tanh(x):
    return 0.5 * x * (1.0 + jnp.tanh(math.sqrt(2.0 / math.pi) * (x + 0.044715 * (x * x * x))))


def _lru_gates(xc, wg_ref, bg_ref, lam_ref):
    w = xc.shape[-1]
    g = jnp.dot(xc.astype(BF16), wg_ref[...], preferred_element_type=F32) + bg_ref[...]
    r = _sigmoid(g[:, :w])
    ig = _sigmoid(g[:, w:])
    log_a = -LRU_C * r * _softplus(-lam_ref[...])
    a = jnp.exp(log_a)
    u = jnp.sqrt(-jnp.tanh(log_a) * (1.0 + a * a)) * (ig * xc)
    return a, u


def _load_blocked(sc, rows):
    return jnp.concatenate([sc[c, rows, :] for c in range(sc.shape[0])], axis=1)


def _store_blocked(sc, rows, val):
    for c in range(sc.shape[0]):
        sc[c, rows, :] = val[:, c * LANES:(c + 1) * LANES]


def _blocked(rows, w):
    return pltpu.VMEM((w // LANES, rows, LANES), F32)


def _lru_prompt_kernel(xl_ref, gl_ref, cw_ref, cb_ref, wg_ref, bg_ref, lam_ref,
                       y_ref, conv_ref, hlast_ref, xbuf, a_sc, u_sc, hcar):
    t = pl.program_id(1)
    nt = pl.num_programs(1)
    tm, w = xl_ref.shape
    seg = tm // LRU_SEGMENTS
    pitch = seg + LRU_SEG_PAD
    hist = CONV_WIDTH - 1

    @pl.when(t == 0)
    def _init():
        xbuf[0:SUBLANES, :] = jnp.zeros((SUBLANES, w), F32)
        hcar[...] = jnp.zeros(hcar.shape, F32)

    xbuf[SUBLANES:SUBLANES + tm, :] = xl_ref[...]
    xc = cb_ref[...]
    for jj in range(CONV_WIDTH):
        xc = xc + xbuf[pl.ds(SUBLANES - hist + jj, tm), :] * cw_ref[jj:jj + 1, :]
    a, u = _lru_gates(xc, wg_ref, bg_ref, lam_ref)
    for s in range(LRU_SEGMENTS):
        _store_blocked(a_sc, slice(s * pitch, s * pitch + seg), a[s * seg:(s + 1) * seg])
        _store_blocked(u_sc, slice(s * pitch, s * pitch + seg), u[s * seg:(s + 1) * seg])

    def step(i, carry):
        p, hh = carry
        rows = pl.ds(i, LRU_SEGMENTS, stride=pitch)
        ai = _load_blocked(a_sc, rows)
        p = ai * p
        hh = ai * hh + _load_blocked(u_sc, rows)
        _store_blocked(a_sc, rows, p)
        _store_blocked(u_sc, rows, hh)
        return p, hh

    p_end, h_end = lax.fori_loop(0, seg, step,
                                 (jnp.ones((LRU_SEGMENTS, w), F32), jnp.zeros((LRU_SEGMENTS, w), F32)))

    h_in = hcar[...]
    for s in range(LRU_SEGMENTS):
        blk = slice(s * seg, (s + 1) * seg)
        sblk = slice(s * pitch, s * pitch + seg)
        hs = _load_blocked(u_sc, sblk) + _load_blocked(a_sc, sblk) * h_in
        y_ref[blk, :] = (hs * _gelu_tanh(gl_ref[blk, :])).astype(y_ref.dtype)
        h_in = p_end[s:s + 1, :] * h_in + h_end[s:s + 1, :]
    hcar[...] = h_in
    xbuf[0:SUBLANES, :] = xbuf[tm:tm + SUBLANES, :]

    @pl.when(t == nt - 1)
    def _fin():
        hlast_ref[...] = h_in
        conv_ref[...] = xbuf[pl.ds(SUBLANES - hist, hist), :]


def _lru_prompt(xl, gl, conv_w, conv_b, wg_bf, bg, lru_lambda, b, s, tm):
    w = xl.shape[-1]
    nt = s // tm
    hist = CONV_WIDTH - 1
    scan_rows = LRU_SEGMENTS * (tm // LRU_SEGMENTS + LRU_SEG_PAD)
    full = lambda shape: pl.BlockSpec(shape, lambda i, j: (0,) * len(shape))
    return pl.pallas_call(
        _lru_prompt_kernel,
        grid=(b, nt),
        in_specs=[pl.BlockSpec((tm, w), lambda i, j: (i * nt + j, 0)),
                  pl.BlockSpec((tm, w), lambda i, j: (i * nt + j, 0)),
                  full((CONV_WIDTH, w)), full((1, w)), full(wg_bf.shape), full((1, 2 * w)),
                  full((1, w))],
        out_specs=[pl.BlockSpec((tm, w), lambda i, j: (i * nt + j, 0)),
                   pl.BlockSpec((None, hist, w), lambda i, j: (i, 0, 0)),
                   pl.BlockSpec((None, 1, w), lambda i, j: (i, 0, 0))],
        out_shape=[jax.ShapeDtypeStruct((b * s, w), BF16),
                   jax.ShapeDtypeStruct((b, hist, w), F32),
                   jax.ShapeDtypeStruct((b, 1, w), F32)],
        scratch_shapes=[pltpu.VMEM((tm + SUBLANES, w), F32), _blocked(scan_rows, w),
                        _blocked(scan_rows, w), pltpu.VMEM((1, w), F32)],
        compiler_params=_params("arbitrary", "arbitrary"),
        name="lru_prompt",
    )(xl, gl, conv_w, conv_b.reshape(1, w), wg_bf, bg.reshape(1, 2 * w), lru_lambda.reshape(1, w))


def _lru_sample_kernel(xl_ref, gl_ref, cbuf_ref, h0_ref, cw_ref, cb_ref, wg_ref, bg_ref, lam_ref,
                       y_ref, conv_ref, hlast_ref, x_sc, g_sc, c_sc, y_sc, *, t_new):
    nb = h0_ref.shape[0]
    hist = CONV_WIDTH - 1
    _store_blocked(x_sc, slice(None), xl_ref[...])
    _store_blocked(g_sc, slice(None), gl_ref[...])
    _store_blocked(c_sc, slice(None), cbuf_ref[...])
    xp = [_load_blocked(c_sc, pl.ds(jj, nb, stride=hist)) for jj in range(hist)]
    xp += [_load_blocked(x_sc, pl.ds(tt, nb, stride=t_new)) for tt in range(t_new)]
    hh = h0_ref[...]
    for tt in range(t_new):
        xc = cb_ref[...]
        for jj in range(CONV_WIDTH):
            xc = xc + xp[tt + jj] * cw_ref[jj:jj + 1, :]
        a, u = _lru_gates(xc, wg_ref, bg_ref, lam_ref)
        hh = a * hh + u
        gate = _gelu_tanh(_load_blocked(g_sc, pl.ds(tt, nb, stride=t_new)))
        _store_blocked(y_sc, pl.ds(tt, nb, stride=t_new), hh * gate)
    hlast_ref[...] = hh
    y_ref[...] = _load_blocked(y_sc, slice(None)).astype(y_ref.dtype)
    for jj in range(hist):
        _store_blocked(c_sc, pl.ds(jj, nb, stride=hist), xp[t_new + jj])
    conv_ref[...] = _load_blocked(c_sc, slice(None))


def _lru_sample(xl, gl, conv_buf, h0, conv_w, conv_b, wg_bf, bg, lru_lambda, t_new):
    n_tok, w = xl.shape
    nb = n_tok // t_new
    hist = CONV_WIDTH - 1
    full = lambda shape: pl.BlockSpec(shape, lambda i: (0,) * len(shape))
    return pl.pallas_call(
        functools.partial(_lru_sample_kernel, t_new=t_new),
        grid=(1,),
        in_specs=[full((n_tok, w)), full((n_tok, w)), full((nb * hist, w)), full((nb, w)),
                  full((CONV_WIDTH, w)), full((1, w)), full(wg_bf.shape), full((1, 2 * w)),
                  full((1, w))],
        out_specs=[full((n_tok, w)), full((nb * hist, w)), full((nb, w))],
        out_shape=[jax.ShapeDtypeStruct((n_tok, w), BF16),
                   jax.ShapeDtypeStruct((nb * hist, w), F32),
                   jax.ShapeDtypeStruct((nb, w), F32)],
        scratch_shapes=[_blocked(n_tok, w), _blocked(n_tok, w), _blocked(nb * hist, w),
                        _blocked(n_tok, w)],
        compiler_params=_params("arbitrary"),
        name="lru_sample",
    )(xl, gl, conv_buf.reshape(nb * hist, w), h0, conv_w, conv_b.reshape(1, w), wg_bf,
      bg.reshape(1, 2 * w), lru_lambda.reshape(1, w))


def _mlp_kernel(x_ref, o_ref, y_ref, gt1_ref, sc2_ref, sh2_ref, gt2_ref, g2_ref, gf_ref,
                wout_ref, w1_ref, w2_ref, out_ref, *, ff_chunk, final_norm):
    bb, tt, d = x_ref.shape
    rows = bb * tt
    mix_in = jnp.concatenate([o_ref[...], y_ref[...]], axis=1)
    mix = jnp.dot(mix_in, wout_ref[...], preferred_element_type=F32).reshape(bb, tt, d)
    x1 = x_ref[...] + gt1_ref[...] * mix
    h2 = (_rms(x1, g2_ref[...]) * (1.0 + sc2_ref[...]) + sh2_ref[...]).reshape(rows, d).astype(BF16)
    d_ff = w1_ref.shape[1]
    ff = jnp.zeros((rows, d), F32)
    for c in range(d_ff // ff_chunk):
        cs = slice(c * ff_chunk, (c + 1) * ff_chunk)
        hc = jnp.dot(h2, w1_ref[:, cs], preferred_element_type=F32)
        hc = jnp.square(jnp.maximum(hc, 0.0)).astype(BF16)
        ff = ff + jnp.dot(hc, w2_ref[cs, :], preferred_element_type=F32)
    x2 = x1 + gt2_ref[...] * ff.reshape(bb, tt, d)
    out_ref[...] = _rms(x2, gf_ref[...]) if final_norm else x2


def _mlp(x, o, y, gt1, sc2, sh2, gt2, g2, gf, wout_bf, w1_bf, w2_bf, bb, tt, final_norm):
    b, t, d = x.shape
    nt = t // tt
    rows = bb * tt
    mix_w = o.shape[-1]
    mod = pl.BlockSpec((bb, 1, d), lambda i, j: (i, 0, 0))
    gain = pl.BlockSpec((1, 1, d), lambda i, j: (0, 0, 0))
    wspec = lambda w: pl.BlockSpec(w.shape, lambda i, j: (0, 0), pipeline_mode=pl.Buffered(1))
    return pl.pallas_call(
        functools.partial(_mlp_kernel, ff_chunk=min(1024, w1_bf.shape[1]), final_norm=final_norm),
        grid=(b // bb, nt),
        in_specs=[pl.BlockSpec((bb, tt, d), lambda i, j: (i, j, 0)),
                  pl.BlockSpec((rows, mix_w), lambda i, j: (i * nt + j, 0)),
                  pl.BlockSpec((rows, y.shape[-1]), lambda i, j: (i * nt + j, 0)),
                  mod, mod, mod, mod, gain, gain, wspec(wout_bf), wspec(w1_bf), wspec(w2_bf)],
        out_specs=pl.BlockSpec((bb, tt, d), lambda i, j: (i, j, 0)),
        out_shape=jax.ShapeDtypeStruct((b, t, d), F32),
        compiler_params=_params("arbitrary", "arbitrary"),
        name="out_mlp",
    )(x, o, y, gt1, sc2, sh2, gt2, g2.reshape(1, 1, d), gf.reshape(1, 1, d), wout_bf, w1_bf, w2_bf)


def _block_diag(w):
    h, i, j = w.shape
    eye = jnp.eye(h, dtype=w.dtype)
    return (eye[:, None, :, None] * w[:, :, None, :]).reshape(h * i, h * j)


def _pick(n, target):
    t = min(n, target)
    while n % t:
        t -= 1
    return t


def kernel(x_prompt, x_sample, c_prompt, c_sample, cache_k, cache_v, page_table, state_h, state_conv, w_ada, b_ada, g_norm1, g_norm2, w_in, lambda_q1, lambda_k1, lambda_q2, lambda_k2, g_subln, conv_w, conv_b, w_rg, b_rg, w_ig, b_ig, lru_lambda, w_out, w_ff1, w_ff2, g_final):
    depth = w_in.shape[0]
    bp, seq, d = x_prompt.shape
    bs, t_new, _ = x_sample.shape
    lru_w = conv_w.shape[-1]
    da_w = (w_in.shape[-1] - 2 * lru_w) // 3
    dv = da_w // N_DA_HEADS
    n_pool, page_size = cache_k.shape[1], cache_k.shape[2]
    n_pages = page_table.shape[1]
    hist = CONV_WIDTH - 1

    slopes = 2.0 ** (-8.0 * jnp.arange(1, N_DA_HEADS + 1, dtype=F32) / N_DA_HEADS)
    cache_k2 = cache_k.reshape(depth * n_pool, page_size * N_DA_HEADS, dv)
    cache_v2 = cache_v.reshape(depth * n_pool, page_size * N_DA_HEADS, dv)

    tt_p = _pick(seq, 512)
    bb_s = _pick(bs, 64)
    tq = _pick(seq, 256)
    n_pg = _pick(n_pages, 16)
    tm_lru = _pick(seq, 512)

    yp, ys = x_prompt, x_sample
    kp_l, vp_l, cp_l, hp_l, ks_l, vs_l, cs_l, hs_l = [], [], [], [], [], [], [], []
    for l in range(depth):
        lam0 = _lambda_init(l)
        w_in_bf = w_in[l].astype(BF16)
        w_out_bf = w_out[l].astype(BF16)
        w1_bf = w_ff1[l].astype(BF16)
        w2_bf = w_ff2[l].astype(BF16)
        wg_bf = jnp.concatenate([_block_diag(w_rg[l]), _block_diag(w_ig[l])], axis=1).astype(BF16)
        bg = jnp.concatenate([b_rg[l], b_ig[l]])
        lamp = jnp.stack([lambda_q1[l], lambda_k1[l], lambda_q2[l], lambda_k2[l]])
        final = l == depth - 1

        mod = _ada_mod(jnp.concatenate([c_prompt, c_sample], axis=0), w_ada[l], b_ada[l])
        mods_p = [m[:, None, :] for m in jnp.split(mod[:bp], 6, axis=-1)]
        mods_s = [m[:, None, :] for m in jnp.split(mod[bp:], 6, axis=-1)]

        qt, kp, vp, kb, vt, xl_p, gl_p = _in_proj(yp, mods_p[1], mods_p[0], g_norm1[l], w_in_bf,
                                                  1, tt_p, da_w, lru_w, tk=tq)
        q, ks, vs, xl_s, gl_s = _in_proj(ys, mods_s[1], mods_s[0], g_norm1[l], w_in_bf,
                                         bb_s, t_new, da_w, lru_w)
        o_s, o_p = _attention(
            q.reshape(bs, t_new, da_w), ks.reshape(bs, t_new * N_DA_HEADS, dv),
            vs.reshape(bs, t_new * N_DA_HEADS, dv), cache_k2, cache_v2, page_table, l * n_pool,
            qt, kb.reshape(bp, seq, da_w), vt, lamp, g_subln[l], slopes, lam0, n_pg, tq)

        y_lru, cp, hp = _lru_prompt(xl_p, gl_p, conv_w[l], conv_b[l], wg_bf, bg, lru_lambda[l],
                                    bp, seq, tm_lru)
        yp = _mlp(yp, o_p.reshape(bp * seq, da_w), y_lru, mods_p[2], mods_p[4], mods_p[3],
                  mods_p[5], g_norm2[l], g_final, w_out_bf, w1_bf, w2_bf, 1, tt_p, final)
        kp_l.append(kp.reshape(bp, seq, N_DA_HEADS, dv))
        vp_l.append(vp.reshape(bp, seq, N_DA_HEADS, dv))
        cp_l.append(cp)
        hp_l.append(hp.reshape(bp, lru_w))

        y_lru, cs, hs = _lru_sample(xl_s, gl_s, state_conv[l], state_h[l], conv_w[l], conv_b[l],
                                    wg_bf, bg, lru_lambda[l], t_new)
        ys = _mlp(ys, o_s.reshape(bs * t_new, da_w), y_lru, mods_s[2], mods_s[4], mods_s[3],
                  mods_s[5], g_norm2[l], g_final, w_out_bf, w1_bf, w2_bf, bb_s, t_new, final)
        ks_l.append(ks.reshape(bs, t_new, N_DA_HEADS, dv))
        vs_l.append(vs.reshape(bs, t_new, N_DA_HEADS, dv))
        cs_l.append(cs.reshape(bs, hist, lru_w))
        hs_l.append(hs)

    return (yp, ys,
            jnp.stack(kp_l), jnp.stack(vp_l), jnp.stack(cp_l), jnp.stack(hp_l),
            jnp.stack(ks_l), jnp.stack(vs_l), jnp.stack(cs_l), jnp.stack(hs_l))
```

```python
import functools
import math

import jax
import jax.numpy as jnp
import numpy as np
from jax import lax
from jax.experimental import pallas as pl
from jax.experimental.pallas import tpu as pltpu

F32 = jnp.float32
BF16 = jnp.bfloat16

N_DA_HEADS = 4
N_LRU_HEADS = 8
CONV_WIDTH = 4
LRU_C = 8.0
EPS = 1e-6
NEG_BIG = -1e30
SUBLANES = 8
LANES = 128
MXU_DIM = 256
VMEM_LIMIT_BYTES = 56 * 1024 * 1024
SAMPLE_RING_SLOTS = 3
LRU_SEGMENTS = SUBLANES
LRU_SEG_PAD = SUBLANES


def _lambda_init(layer):
    return 0.8 - 0.6 * math.exp(-0.3 * layer)


def _params(*sem):
    return pltpu.CompilerParams(dimension_semantics=sem, vmem_limit_bytes=VMEM_LIMIT_BYTES)


def _rms(x, g):
    return x * lax.rsqrt(jnp.mean(x * x, axis=-1, keepdims=True) + EPS) * g


def _ada_kernel(c_ref, w_ref, b_ref, o_ref):
    c = c_ref[...]
    s = (c * jax.nn.sigmoid(c)).astype(BF16)
    o_ref[...] = jnp.dot(s, w_ref[...].astype(BF16), preferred_element_type=F32) + b_ref[...]


def _ada_mod(c, w, b, tn=1536):
    m, d = c.shape
    n = w.shape[1]
    return pl.pallas_call(
        _ada_kernel,
        grid=(n // tn,),
        in_specs=[pl.BlockSpec((m, d), lambda j: (0, 0)),
                  pl.BlockSpec((d, tn), lambda j: (0, j)),
                  pl.BlockSpec((1, tn), lambda j: (0, j))],
        out_specs=pl.BlockSpec((m, tn), lambda j: (0, j)),
        out_shape=jax.ShapeDtypeStruct((m, n), F32),
        compiler_params=_params("arbitrary"),
        name="ada_mod",
    )(c, w, b.reshape(1, n))


def _inproj_kernel(x_ref, sc_ref, sh_ref, g_ref, w_ref, *refs, da_w, lru_w, q_scale, tk):
    bb, tt, d = x_ref.shape
    rows = bb * tt
    dv = da_w // N_DA_HEADS
    h = _rms(x_ref[...], g_ref[...]) * (1.0 + sc_ref[...]) + sh_ref[...]
    h = h.reshape(rows, d).astype(BF16)
    proj = jnp.dot(h, w_ref[...], preferred_element_type=F32)
    q = proj[:, :da_w] * q_scale
    k = proj[:, da_w:2 * da_w]
    v = proj[:, 2 * da_w:3 * da_w]
    if tk is None:
        q_ref, k4_ref, v4_ref, xl_ref, gl_ref = refs
        q_ref[...] = q.astype(BF16)
    else:
        qt_ref, k4_ref, v4_ref, kb_ref, vt_ref, xl_ref, gl_ref = refs
        qt_ref[...] = q.T.astype(BF16)
        kb_ref[...] = k.astype(BF16)
        vt = v.T.astype(BF16)
        for c in range(rows // tk):
            vt_ref[c] = vt[:, c * tk:(c + 1) * tk]
    for hh in range(N_DA_HEADS):
        k4_ref[pl.ds(hh, rows, stride=N_DA_HEADS), :] = k[:, hh * dv:(hh + 1) * dv]
        v4_ref[pl.ds(hh, rows, stride=N_DA_HEADS), :] = v[:, hh * dv:(hh + 1) * dv]
    xl_ref[...] = proj[:, 3 * da_w:3 * da_w + lru_w]
    gl_ref[...] = proj[:, 3 * da_w + lru_w:]


def _in_proj(x, sc, sh, g, w_bf, bb, tt, da_w, lru_w, tk=None):
    b, t, d = x.shape
    n_tok = b * t
    nt = t // tt
    rows = bb * tt
    dv = da_w // N_DA_HEADS
    dqk = dv // 2

    def flat(width, dtype, mult=1):
        return (pl.BlockSpec((rows * mult, width), lambda i, j: (i * nt + j, 0)),
                jax.ShapeDtypeStruct((n_tok * mult, width), dtype))

    kv4 = [flat(dv, F32, N_DA_HEADS), flat(dv, F32, N_DA_HEADS)]
    lru = [flat(lru_w, F32), flat(lru_w, F32)]
    if tk is None:
        outs = [flat(da_w, BF16)] + kv4 + lru
    else:
        assert bb == 1 and tt % tk == 0
        qt = (pl.BlockSpec((None, da_w, tt), lambda i, j: (i, 0, j)),
              jax.ShapeDtypeStruct((b, da_w, t), BF16))
        vt = (pl.BlockSpec((None, tt // tk, da_w, tk), lambda i, j: (i, j, 0, 0)),
              jax.ShapeDtypeStruct((b, t // tk, da_w, tk), BF16))
        outs = [qt] + kv4 + [flat(da_w, BF16), vt] + lru
    return pl.pallas_call(
        functools.partial(_inproj_kernel, da_w=da_w, lru_w=lru_w, q_scale=dqk ** -0.5, tk=tk),
        grid=(b // bb, nt),
        in_specs=[pl.BlockSpec((bb, tt, d), lambda i, j: (i, j, 0)),
                  pl.BlockSpec((bb, 1, d), lambda i, j: (i, 0, 0)),
                  pl.BlockSpec((bb, 1, d), lambda i, j: (i, 0, 0)),
                  pl.BlockSpec((1, 1, d), lambda i, j: (0, 0, 0)),
                  pl.BlockSpec(w_bf.shape, lambda i, j: (0, 0))],
        out_specs=[o[0] for o in outs],
        out_shape=[o[1] for o in outs],
        compiler_params=_params("arbitrary", "arbitrary"),
        name="in_proj",
    )(x, sc, sh, g.reshape(1, 1, d), w_bf)


def _lam_value(lamp_ref, lam0):
    lp = lamp_ref[...]
    t1 = jnp.sum(lp[0:1] * lp[1:2], axis=-1, keepdims=True)
    t2 = jnp.sum(lp[2:3] * lp[3:4], axis=-1, keepdims=True)
    return jnp.exp(t1) - jnp.exp(t2) + lam0


def _softmax_update(s, m_sc, l_sc, acc_sc, pv_fn):
    m_old = m_sc[...]
    m_new = jnp.maximum(m_old, jnp.max(s, axis=-1, keepdims=True))
    alpha = jnp.exp(m_old - m_new)
    p = jnp.exp(s - m_new)
    l_sc[...] = alpha * l_sc[...] + jnp.sum(p, axis=-1, keepdims=True)
    acc_sc[...] = alpha * acc_sc[...] + pv_fn(p.astype(BF16))
    m_sc[...] = m_new


def _nt_dot(a, b):
    return lax.dot_general(a, b, (((1,), (1,)), ((), ())), preferred_element_type=F32)


(WK_ATT_RUN, WK_ATT_B, WK_ATT_QI, WK_ATT_J, WK_ATT_OUT_B, WK_ATT_OUT_QI, WORK_COLS) = range(7)


def _prompt_work_table(n_steps, n_batch, n_qt):
    units = [(b, qi, j) for b in range(n_batch) for qi in range(n_qt) for j in range(qi + 1)]
    assert len(units) <= n_steps, "prompt attention units must fit the sample-attention grid"
    step_of = {u * n_steps // len(units): u for u in range(len(units))}
    table = np.zeros((n_steps, WORK_COLS), np.int32)
    nxt, done = 0, 0
    for g in range(n_steps):
        u = step_of.get(g)
        if u is not None:
            nxt, done = u + 1, u
        b_in, qi_in, j_in = units[u if u is not None else min(nxt, len(units) - 1)]
        table[g] = (u is not None, b_in, qi_in, j_in) + units[done][:2]
    return table.reshape(-1)


def _prompt_unit(qi, j, slopes_ref, qt_ref, k_ref, vt_ref, lamp_ref, g_ref, o_ref,
                 m_sc, l_sc, acc_sc, s_sc, qst_sc, *, tq, lam0):
    dv = g_ref.shape[0]
    heads = range(N_DA_HEADS)
    cols = [slice(h * dv, (h + 1) * dv) for h in heads]
    key = lax.broadcasted_iota(jnp.int32, (tq, LANES), 0).astype(F32)

    def scores(jj, slot):
        start = pl.multiple_of(jj * tq, tq)
        for h in heads:
            s_sc[slot, h] = jnp.dot(k_ref[pl.ds(start, tq), cols[h]], qst_sc[h],
                                    preferred_element_type=F32)

    def update(slot, masked):
        if masked:
            kk = lax.broadcasted_iota(jnp.int32, (tq, 2 * tq), 0)
            qq = lax.broadcasted_iota(jnp.int32, (tq, 2 * tq), 1)
            visible = kk <= jnp.where(qq >= tq, qq - tq, qq)
        tile_off = ((j - qi) * tq).astype(F32)
        for h in heads:
            s = s_sc[slot, h] + jnp.concatenate([slopes_ref[h] * key] * (2 * tq // LANES), axis=1)
            if masked:
                s = jnp.where(visible, s, NEG_BIG)
            shift = slopes_ref[h] * tile_off
            m_old = m_sc[h]
            m_new = jnp.maximum(m_old, jnp.max(s, axis=0, keepdims=True) + shift)
            alpha = jnp.exp(m_old - m_new)
            p = jnp.exp(s - (m_new - shift))
            l_sc[h] = alpha * l_sc[h] + jnp.sum(p, axis=0, keepdims=True)
            acc_sc[h] = alpha * acc_sc[h] + jnp.dot(vt_ref[j, cols[h], :], p.astype(BF16),
                                                    preferred_element_type=F32)
            m_sc[h] = m_new

    @pl.when(j == 0)
    def _start():
        row = lax.broadcasted_iota(jnp.int32, (dv, tq), 0)
        zero = jnp.zeros((dv, tq), BF16)
        for h in heads:
            qt = qt_ref[cols[h], :]
            qst_sc[h] = jnp.concatenate([jnp.where(row < dv // 2, qt, zero),
                                         jnp.where(row >= dv // 2, qt, zero)], axis=1)
        m_sc[...] = jnp.full(m_sc.shape, NEG_BIG, F32)
        l_sc[...] = jnp.zeros(l_sc.shape, F32)
        acc_sc[...] = jnp.zeros(acc_sc.shape, F32)
        scores(0, 0)

    for parity in (0, 1):
        @pl.when((j < qi) & (j % 2 == parity))
        def _off_diagonal():
            scores(j + 1, 1 - parity)
            update(parity, False)

        @pl.when((j == qi) & (j % 2 == parity))
        def _diagonal():
            update(parity, True)

    @pl.when(j == qi)
    def _finish():
        lam = _lam_value(lamp_ref, lam0)
        outs = []
        for h in heads:
            on = acc_sc[h] * (1.0 / l_sc[h])
            ot = on[:, :tq] - lam * on[:, tq:]
            ms = jnp.mean(ot * ot, axis=0, keepdims=True)
            outs.append((ot * lax.rsqrt(ms + EPS) * g_ref[...] * (1.0 - lam0)).T)
        o_ref[...] = jnp.concatenate(outs, axis=1).astype(o_ref.dtype)


def _attn_kernel(pt_ref, slopes_ref, work_ref,
                 q_ref, kn_ref, vn_ref, lamp_ref, g_ref, kc_hbm, vc_hbm,
                 qt_ref, kp_ref, vt_ref, gcol_ref,
                 o_ref, op_ref,
                 kbuf, vbuf, sem, m_sc, l_sc, acc_sc,
                 pm_sc, pl_sc, pacc_sc, ps_sc, qst_sc,
                 *, n_seq, nj, n_pg, pg_rows, past, lam0, pool_off, tq):
    seq = pl.program_id(0)
    j = pl.program_id(1)
    n_chunks = n_seq * nj
    n_slots = kbuf.shape[0]
    g = seq * nj + j
    t_new, da_w = q_ref.shape
    dv = da_w // N_DA_HEADS
    rows_h = 2 * t_new
    n_rows = N_DA_HEADS * rows_h
    n_keys = n_pg * pg_rows // N_DA_HEADS

    def chunk_copies(chunk, slot):
        cs = chunk // nj
        cj = chunk % nj
        copies = []
        for i in range(n_pg):
            page = pt_ref[cs, cj * n_pg + i] + pool_off
            rows = pl.ds(i * pg_rows, pg_rows)
            copies.append(pltpu.make_async_copy(kc_hbm.at[page], kbuf.at[slot, rows], sem.at[0, slot]))
            copies.append(pltpu.make_async_copy(vc_hbm.at[page], vbuf.at[slot, rows], sem.at[1, slot]))
        return copies

    @pl.when(g == 0)
    def _prologue():
        for c in range(min(n_slots - 1, n_chunks)):
            for cp in chunk_copies(c, c):
                cp.start()

    ahead = g + (n_slots - 1)

    @pl.when(ahead < n_chunks)
    def _prefetch():
        for cp in chunk_copies(ahead, ahead % n_slots):
            cp.start()

    work = lambda col: work_ref[g * WORK_COLS + col]

    @pl.when(work(WK_ATT_RUN) == 1)
    def _prompt_attention():
        _prompt_unit(work(WK_ATT_QI), work(WK_ATT_J), slopes_ref,
                     qt_ref, kp_ref, vt_ref, lamp_ref, gcol_ref, op_ref,
                     pm_sc, pl_sc, pacc_sc, ps_sc, qst_sc, tq=tq, lam0=lam0)

    slot = g % n_slots
    for cp in chunk_copies(g, slot):
        cp.wait()

    row1 = lax.broadcasted_iota(jnp.int32, (n_rows, 1), 0)
    h_row = row1 // rows_h
    q_row = row1 % t_new
    slope_row = jnp.exp2(-8.0 * (h_row + 1).astype(F32) / N_DA_HEADS)

    qf = q_ref[...].astype(F32)
    lane = lax.broadcasted_iota(jnp.int32, (t_new, dv), 1)
    pieces = []
    for h in range(N_DA_HEADS):
        qh = qf[:, h * dv:(h + 1) * dv]
        pieces.append(jnp.where(lane < dv // 2, qh, 0.0))
        pieces.append(jnp.where(lane >= dv // 2, qh, 0.0))
    qall = jnp.concatenate(pieces, axis=0).astype(BF16)

    @pl.when(j == 0)
    def _init():
        m_sc[...] = jnp.full(m_sc.shape, NEG_BIG, F32)
        l_sc[...] = jnp.zeros(l_sc.shape, F32)
        acc_sc[...] = jnp.zeros(acc_sc.shape, F32)

    grp = MXU_DIM // dv
    n_grp = N_DA_HEADS // grp

    def group_rows(buf, gi):
        return jnp.concatenate([buf[slot, pl.ds(gi * grp + hh, n_keys, stride=N_DA_HEADS), :]
                                for hh in range(grp)], axis=1).astype(BF16)

    zq = jnp.zeros((rows_h, dv), F32)
    s_parts = []
    for gi in range(n_grp):
        qg = jnp.concatenate(
            [jnp.concatenate([jnp.concatenate(pieces[2 * (gi * grp + hh):2 * (gi * grp + hh) + 2], axis=0)
                              if cc == hh else zq for cc in range(grp)], axis=1)
             for hh in range(grp)], axis=0).astype(BF16)
        s_parts.append(_nt_dot(qg, group_rows(kbuf, gi)))
    s = jnp.concatenate(s_parts, axis=0)
    kpos = lax.broadcasted_iota(jnp.int32, (1, n_keys), 1) + (j * n_keys - past)
    s = s + slope_row * kpos.astype(F32)

    def pv(p):
        outs = []
        for gi in range(n_grp):
            og = jnp.dot(p[gi * grp * rows_h:(gi + 1) * grp * rows_h], group_rows(vbuf, gi),
                         preferred_element_type=F32)
            outs += [og[hh * rows_h:(hh + 1) * rows_h, hh * dv:(hh + 1) * dv] for hh in range(grp)]
        return jnp.concatenate(outs, axis=0)

    _softmax_update(s, m_sc, l_sc, acc_sc, pv)

    @pl.when(j == nj - 1)
    def _finish():
        n_new = kn_ref.shape[0]
        pad = jnp.zeros((LANES - n_new, dv), F32)
        kn = jnp.concatenate([kn_ref[...], pad], axis=0).astype(BF16)
        vn = jnp.concatenate([vn_ref[...], pad], axis=0).astype(BF16)
        c = lax.broadcasted_iota(jnp.int32, (n_rows, LANES), 1)
        key = c // N_DA_HEADS
        ok = (c % N_DA_HEADS == h_row) & (key <= q_row) & (c < n_new)
        sn = jnp.where(ok, _nt_dot(qall, kn) + slope_row * key.astype(F32), NEG_BIG)
        _softmax_update(sn, m_sc, l_sc, acc_sc,
                        lambda p: jnp.dot(p, vn, preferred_element_type=F32))

        lam = _lam_value(lamp_ref, lam0)
        on = acc_sc[...] / l_sc[...]
        outs = []
        for h in range(N_DA_HEADS):
            o1 = on[h * rows_h:h * rows_h + t_new]
            o2 = on[h * rows_h + t_new:(h + 1) * rows_h]
            outs.append(_rms(o1 - lam * o2, g_ref[...]) * (1.0 - lam0))
        o_ref[...] = jnp.concatenate(outs, axis=1).astype(o_ref.dtype)


def _attention(q, k_new, v_new, cache_k2, cache_v2, page_table, pool_off, qt, kb, vt, lamp, g_subln,
               slopes, lam0, n_pg, tq):
    b, t_new, da_w = q.shape
    bp, s, _ = kb.shape
    dv = da_w // N_DA_HEADS
    n_pages = page_table.shape[1]
    pg_rows = cache_k2.shape[1]
    past = n_pages * (pg_rows // N_DA_HEADS)
    n_rows = N_DA_HEADS * 2 * t_new
    nj = n_pages // n_pg
    work = jnp.asarray(_prompt_work_table(b * nj, bp, s // tq))
    kernel = functools.partial(_attn_kernel, n_seq=b, nj=nj, n_pg=n_pg, pg_rows=pg_rows, past=past,
                               lam0=lam0, pool_off=pool_off, tq=tq)

    def wk(col):
        return lambda i, j, pt, sl, w: w[(i * nj + j) * WORK_COLS + col]

    b_in, qi_in, b_out, qi_out = wk(WK_ATT_B), wk(WK_ATT_QI), wk(WK_ATT_OUT_B), wk(WK_ATT_OUT_QI)
    seq_blk = lambda i, j, pt, sl, w: (i, 0, 0)
    const2 = lambda i, j, pt, sl, w: (0, 0)
    in_specs = [pl.BlockSpec((None, t_new, da_w), seq_blk),
                pl.BlockSpec((None, t_new * N_DA_HEADS, dv), seq_blk),
                pl.BlockSpec((None, t_new * N_DA_HEADS, dv), seq_blk),
                pl.BlockSpec(lamp.shape, const2),
                pl.BlockSpec((1, dv), const2),
                pl.BlockSpec(memory_space=pl.ANY),
                pl.BlockSpec(memory_space=pl.ANY),
                pl.BlockSpec((None, da_w, tq), lambda *a: (b_in(*a), 0, qi_in(*a))),
                pl.BlockSpec((None, s, da_w), lambda *a: (b_in(*a), 0, 0)),
                pl.BlockSpec((None, s // tq, da_w, tq), lambda *a: (b_in(*a), 0, 0, 0)),
                pl.BlockSpec((dv, 1), const2)]
    out_specs = [pl.BlockSpec((None, t_new, da_w), seq_blk),
                 pl.BlockSpec((None, tq, da_w), lambda *a: (b_out(*a), qi_out(*a), 0))]
    ring = pltpu.VMEM((SAMPLE_RING_SLOTS, n_pg * pg_rows, dv), F32)
    return pl.pallas_call(
        kernel,
        grid_spec=pltpu.PrefetchScalarGridSpec(
            num_scalar_prefetch=3,
            grid=(b, nj),
            in_specs=in_specs,
            out_specs=out_specs,
            scratch_shapes=[ring, ring, pltpu.SemaphoreType.DMA((2, SAMPLE_RING_SLOTS)),
                            pltpu.VMEM((n_rows, 1), F32), pltpu.VMEM((n_rows, 1), F32),
                            pltpu.VMEM((n_rows, dv), F32),
                            pltpu.VMEM((N_DA_HEADS, 1, 2 * tq), F32),
                            pltpu.VMEM((N_DA_HEADS, 1, 2 * tq), F32),
                            pltpu.VMEM((N_DA_HEADS, dv, 2 * tq), F32),
                            pltpu.VMEM((2, N_DA_HEADS, tq, 2 * tq), F32),
                            pltpu.VMEM((N_DA_HEADS, dv, 2 * tq), BF16)]),
        out_shape=[jax.ShapeDtypeStruct((b, t_new, da_w), BF16),
                   jax.ShapeDtypeStruct((bp, s, da_w), BF16)],
        compiler_params=_params("arbitrary", "arbitrary"),
        name="attention",
    )(page_table, slopes, work, q, k_new, v_new, lamp, g_subln.reshape(1, dv), cache_k2, cache_v2,
      qt, kb, vt, g_subln.reshape(dv, 1))


def _softplus(z):
    return jnp.maximum(z, 0.0) + jnp.log1p(jnp.exp(-jnp.abs(z)))


def _sigmoid(x):
    return 0.5 * jnp.tanh(0.5 * x) + 0.5


def _gelu_tanh(x):
    c = math.sqrt(2.0 / math.pi)
    return x * (0.5 + 0.5 * jnp.tanh(x * (c + (0.044715 * c) * (x * x))))


def _lru_gates(xc, wg_ref, bg_ref, lam_ref):
    w = xc.shape[-1]
    g = jnp.dot(xc.astype(BF16), wg_ref[...], preferred_element_type=F32) + bg_ref[...]
    r = _sigmoid(g[:, :w])
    ig = _sigmoid(g[:, w:])
    log_a = r * (-LRU_C * _softplus(-lam_ref[...]))
    a = jnp.exp(log_a)
    u = jnp.sqrt(-jnp.tanh(log_a) * (1.0 + a * a)) * (ig * xc)
    return a, u


def _load_blocked(sc, rows):
    return jnp.concatenate([sc[c, rows, :] for c in range(sc.shape[0])], axis=1)


def _store_blocked(sc, rows, val):
    for c in range(sc.shape[0]):
        sc[c, rows, :] = val[:, c * LANES:(c + 1) * LANES]


def _blocked(rows, w):
    return pltpu.VMEM((w // LANES, rows, LANES), F32)


def _lru_prompt_kernel(xl_ref, gl_ref, cw_ref, cb_ref, wg_ref, bg_ref, lam_ref,
                       y_ref, conv_ref, hlast_ref, xbuf, a_sc, u_sc, hcar):
    t = pl.program_id(1)
    nt = pl.num_programs(1)
    tm, w = xl_ref.shape
    seg = tm // LRU_SEGMENTS
    pitch = seg + LRU_SEG_PAD
    hist = CONV_WIDTH - 1

    @pl.when(t == 0)
    def _init():
        xbuf[0:SUBLANES, :] = jnp.zeros((SUBLANES, w), F32)
        hcar[...] = jnp.zeros(hcar.shape, F32)

    xbuf[SUBLANES:SUBLANES + tm, :] = xl_ref[...]
    xall = xbuf[...]
    xc = cb_ref[...]
    for jj in range(CONV_WIDTH):
        shifted = xall if jj == hist else pltpu.roll(xall, hist - jj, axis=0)
        xc = xc + shifted[SUBLANES:, :] * cw_ref[jj:jj + 1, :]
    a, u = _lru_gates(xc, wg_ref, bg_ref, lam_ref)
    for s in range(LRU_SEGMENTS):
        _store_blocked(a_sc, slice(s * pitch, s * pitch + seg), a[s * seg:(s + 1) * seg])
        _store_blocked(u_sc, slice(s * pitch, s * pitch + seg), u[s * seg:(s + 1) * seg])

    def step(i, carry):
        p, hh = carry
        rows = pl.ds(i, LRU_SEGMENTS, stride=pitch)
        ai = _load_blocked(a_sc, rows)
        p = ai * p
        hh = ai * hh + _load_blocked(u_sc, rows)
        _store_blocked(a_sc, rows, p)
        _store_blocked(u_sc, rows, hh)
        return p, hh

    p_end, h_end = lax.fori_loop(0, seg, step,
                                 (jnp.ones((LRU_SEGMENTS, w), F32), jnp.zeros((LRU_SEGMENTS, w), F32)))

    h_in = hcar[...]
    for s in range(LRU_SEGMENTS):
        blk = slice(s * seg, (s + 1) * seg)
        sblk = slice(s * pitch, s * pitch + seg)
        hs = _load_blocked(u_sc, sblk) + _load_blocked(a_sc, sblk) * h_in
        y_ref[blk, :] = (hs * _gelu_tanh(gl_ref[blk, :])).astype(y_ref.dtype)
        h_in = p_end[s:s + 1, :] * h_in + h_end[s:s + 1, :]
    hcar[...] = h_in
    xbuf[0:SUBLANES, :] = xbuf[tm:tm + SUBLANES, :]

    @pl.when(t == nt - 1)
    def _fin():
        hlast_ref[...] = h_in
        conv_ref[...] = xbuf[pl.ds(SUBLANES - hist, hist), :]


def _lru_prompt(xl, gl, conv_w, conv_b, wg_bf, bg, lru_lambda, b, s, tm):
    w = xl.shape[-1]
    nt = s // tm
    hist = CONV_WIDTH - 1
    scan_rows = LRU_SEGMENTS * (tm // LRU_SEGMENTS + LRU_SEG_PAD)
    full = lambda shape: pl.BlockSpec(shape, lambda i, j: (0,) * len(shape))
    return pl.pallas_call(
        _lru_prompt_kernel,
        grid=(b, nt),
        in_specs=[pl.BlockSpec((tm, w), lambda i, j: (i * nt + j, 0)),
                  pl.BlockSpec((tm, w), lambda i, j: (i * nt + j, 0)),
                  full((CONV_WIDTH, w)), full((1, w)), full(wg_bf.shape), full((1, 2 * w)),
                  full((1, w))],
        out_specs=[pl.BlockSpec((tm, w), lambda i, j: (i * nt + j, 0)),
                   pl.BlockSpec((None, hist, w), lambda i, j: (i, 0, 0)),
                   pl.BlockSpec((None, 1, w), lambda i, j: (i, 0, 0))],
        out_shape=[jax.ShapeDtypeStruct((b * s, w), BF16),
                   jax.ShapeDtypeStruct((b, hist, w), F32),
                   jax.ShapeDtypeStruct((b, 1, w), F32)],
        scratch_shapes=[pltpu.VMEM((tm + SUBLANES, w), F32), _blocked(scan_rows, w),
                        _blocked(scan_rows, w), pltpu.VMEM((1, w), F32)],
        compiler_params=_params("arbitrary", "arbitrary"),
        name="lru_prompt",
    )(xl, gl, conv_w, conv_b.reshape(1, w), wg_bf, bg.reshape(1, 2 * w), lru_lambda.reshape(1, w))


def _lru_sample_kernel(xl_ref, gl_ref, cbuf_ref, h0_ref, cw_ref, cb_ref, wg_ref, bg_ref, lam_ref,
                       y_ref, conv_ref, hlast_ref, x_sc, g_sc, c_sc, y_sc, *, t_new):
    nb = h0_ref.shape[0]
    hist = CONV_WIDTH - 1
    _store_blocked(x_sc, slice(None), xl_ref[...])
    _store_blocked(g_sc, slice(None), gl_ref[...])
    _store_blocked(c_sc, slice(None), cbuf_ref[...])
    xp = [_load_blocked(c_sc, pl.ds(jj, nb, stride=hist)) for jj in range(hist)]
    xp += [_load_blocked(x_sc, pl.ds(tt, nb, stride=t_new)) for tt in range(t_new)]
    hh = h0_ref[...]
    for tt in range(t_new):
        xc = cb_ref[...]
        for jj in range(CONV_WIDTH):
            xc = xc + xp[tt + jj] * cw_ref[jj:jj + 1, :]
        a, u = _lru_gates(xc, wg_ref, bg_ref, lam_ref)
        hh = a * hh + u
        gate = _gelu_tanh(_load_blocked(g_sc, pl.ds(tt, nb, stride=t_new)))
        _store_blocked(y_sc, pl.ds(tt, nb, stride=t_new), hh * gate)
    hlast_ref[...] = hh
    y_ref[...] = _load_blocked(y_sc, slice(None)).astype(y_ref.dtype)
    for jj in range(hist):
        _store_blocked(c_sc, pl.ds(jj, nb, stride=hist), xp[t_new + jj])
    conv_ref[...] = _load_blocked(c_sc, slice(None))


def _lru_sample(xl, gl, conv_buf, h0, conv_w, conv_b, wg_bf, bg, lru_lambda, t_new):
    n_tok, w = xl.shape
    nb = n_tok // t_new
    hist = CONV_WIDTH - 1
    full = lambda shape: pl.BlockSpec(shape, lambda i: (0,) * len(shape))
    return pl.pallas_call(
        functools.partial(_lru_sample_kernel, t_new=t_new),
        grid=(1,),
        in_specs=[full((n_tok, w)), full((n_tok, w)), full((nb * hist, w)), full((nb, w)),
                  full((CONV_WIDTH, w)), full((1, w)), full(wg_bf.shape), full((1, 2 * w)),
                  full((1, w))],
        out_specs=[full((n_tok, w)), full((nb * hist, w)), full((nb, w))],
        out_shape=[jax.ShapeDtypeStruct((n_tok, w), BF16),
                   jax.ShapeDtypeStruct((nb * hist, w), F32),
                   jax.ShapeDtypeStruct((nb, w), F32)],
        scratch_shapes=[_blocked(n_tok, w), _blocked(n_tok, w), _blocked(nb * hist, w),
                        _blocked(n_tok, w)],
        compiler_params=_params("arbitrary"),
        name="lru_sample",
    )(xl, gl, conv_buf.reshape(nb * hist, w), h0, conv_w, conv_b.reshape(1, w), wg_bf,
      bg.reshape(1, 2 * w), lru_lambda.reshape(1, w))


def _mlp_kernel(x_ref, o_ref, y_ref, gt1_ref, sc2_ref, sh2_ref, gt2_ref, g2_ref, gf_ref,
                wout_ref, w1_ref, w2_ref, out_ref, *, ff_chunk, final_norm):
    bb, tt, d = x_ref.shape
    rows = bb * tt
    mix_in = jnp.concatenate([o_ref[...], y_ref[...]], axis=1)
    mix = jnp.dot(mix_in, wout_ref[...], preferred_element_type=F32).reshape(bb, tt, d)
    x1 = x_ref[...] + gt1_ref[...] * mix
    h2 = (_rms(x1, g2_ref[...]) * (1.0 + sc2_ref[...]) + sh2_ref[...]).reshape(rows, d).astype(BF16)
    d_ff = w1_ref.shape[1]
    ff = jnp.zeros((rows, d), F32)
    for c in range(d_ff // ff_chunk):
        cs = slice(c * ff_chunk, (c + 1) * ff_chunk)
        hc = jnp.dot(h2, w1_ref[:, cs], preferred_element_type=F32)
        hc = jnp.square(jnp.maximum(hc, 0.0)).astype(BF16)
        ff = ff + jnp.dot(hc, w2_ref[cs, :], preferred_element_type=F32)
    x2 = x1 + gt2_ref[...] * ff.reshape(bb, tt, d)
    out_ref[...] = _rms(x2, gf_ref[...]) if final_norm else x2


def _mlp(x, o, y, gt1, sc2, sh2, gt2, g2, gf, wout_bf, w1_bf, w2_bf, bb, tt, final_norm):
    b, t, d = x.shape
    nt = t // tt
    rows = bb * tt
    mix_w = o.shape[-1]
    mod = pl.BlockSpec((bb, 1, d), lambda i, j: (i, 0, 0))
    gain = pl.BlockSpec((1, 1, d), lambda i, j: (0, 0, 0))
    wspec = lambda w: pl.BlockSpec(w.shape, lambda i, j: (0, 0), pipeline_mode=pl.Buffered(1))
    return pl.pallas_call(
        functools.partial(_mlp_kernel, ff_chunk=min(1024, w1_bf.shape[1]), final_norm=final_norm),
        grid=(b // bb, nt),
        in_specs=[pl.BlockSpec((bb, tt, d), lambda i, j: (i, j, 0)),
                  pl.BlockSpec((rows, mix_w), lambda i, j: (i * nt + j, 0)),
                  pl.BlockSpec((rows, y.shape[-1]), lambda i, j: (i * nt + j, 0)),
                  mod, mod, mod, mod, gain, gain, wspec(wout_bf), wspec(w1_bf), wspec(w2_bf)],
        out_specs=pl.BlockSpec((bb, tt, d), lambda i, j: (i, j, 0)),
        out_shape=jax.ShapeDtypeStruct((b, t, d), F32),
        compiler_params=_params("arbitrary", "arbitrary"),
        name="out_mlp",
    )(x, o, y, gt1, sc2, sh2, gt2, g2.reshape(1, 1, d), gf.reshape(1, 1, d), wout_bf, w1_bf, w2_bf)


def _block_diag(w):
    h, i, j = w.shape
    eye = jnp.eye(h, dtype=w.dtype)
    return (eye[:, None, :, None] * w[:, :, None, :]).reshape(h * i, h * j)


def _pick(n, target):
    t = min(n, target)
    while n % t:
        t -= 1
    return t


def kernel(x_prompt, x_sample, c_prompt, c_sample, cache_k, cache_v, page_table, state_h, state_conv, w_ada, b_ada, g_norm1, g_norm2, w_in, lambda_q1, lambda_k1, lambda_q2, lambda_k2, g_subln, conv_w, conv_b, w_rg, b_rg, w_ig, b_ig, lru_lambda, w_out, w_ff1, w_ff2, g_final):
    depth = w_in.shape[0]
    bp, seq, d = x_prompt.shape
    bs, t_new, _ = x_sample.shape
    lru_w = conv_w.shape[-1]
    da_w = (w_in.shape[-1] - 2 * lru_w) // 3
    dv = da_w // N_DA_HEADS
    n_pool, page_size = cache_k.shape[1], cache_k.shape[2]
    n_pages = page_table.shape[1]
    hist = CONV_WIDTH - 1

    slopes = 2.0 ** (-8.0 * jnp.arange(1, N_DA_HEADS + 1, dtype=F32) / N_DA_HEADS)
    cache_k2 = cache_k.reshape(depth * n_pool, page_size * N_DA_HEADS, dv)
    cache_v2 = cache_v.reshape(depth * n_pool, page_size * N_DA_HEADS, dv)

    tt_p = _pick(seq, 512)
    tt_in = _pick(seq, 1024)
    bb_s = _pick(bs, 64)
    tq = _pick(seq, 256)
    n_pg = _pick(n_pages, 16)
    tm_lru = _pick(seq, 512)

    yp, ys = x_prompt, x_sample
    kp_l, vp_l, cp_l, hp_l, ks_l, vs_l, cs_l, hs_l = [], [], [], [], [], [], [], []
    for l in range(depth):
        lam0 = _lambda_init(l)
        w_in_bf = w_in[l].astype(BF16)
        w_out_bf = w_out[l].astype(BF16)
        w1_bf = w_ff1[l].astype(BF16)
        w2_bf = w_ff2[l].astype(BF16)
        wg_bf = jnp.concatenate([_block_diag(w_rg[l]), _block_diag(w_ig[l])], axis=1).astype(BF16)
        bg = jnp.concatenate([b_rg[l], b_ig[l]])
        lamp = jnp.stack([lambda_q1[l], lambda_k1[l], lambda_q2[l], lambda_k2[l]])
        final = l == depth - 1

        mod = _ada_mod(jnp.concatenate([c_prompt, c_sample], axis=0), w_ada[l], b_ada[l])
        mods_p = [m[:, None, :] for m in jnp.split(mod[:bp], 6, axis=-1)]
        mods_s = [m[:, None, :] for m in jnp.split(mod[bp:], 6, axis=-1)]

        qt, kp, vp, kb, vt, xl_p, gl_p = _in_proj(yp, mods_p[1], mods_p[0], g_norm1[l], w_in_bf,
                                                  1, tt_in, da_w, lru_w, tk=tq)
        q, ks, vs, xl_s, gl_s = _in_proj(ys, mods_s[1], mods_s[0], g_norm1[l], w_in_bf,
                                         bb_s, t_new, da_w, lru_w)
        o_s, o_p = _attention(
            q.reshape(bs, t_new, da_w), ks.reshape(bs, t_new * N_DA_HEADS, dv),
            vs.reshape(bs, t_new * N_DA_HEADS, dv), cache_k2, cache_v2, page_table, l * n_pool,
            qt, kb.reshape(bp, seq, da_w), vt, lamp, g_subln[l], slopes, lam0, n_pg, tq)

        y_lru, cp, hp = _lru_prompt(xl_p, gl_p, conv_w[l], conv_b[l], wg_bf, bg, lru_lambda[l],
                                    bp, seq, tm_lru)
        yp = _mlp(yp, o_p.reshape(bp * seq, da_w), y_lru, mods_p[2], mods_p[4], mods_p[3],
                  mods_p[5], g_norm2[l], g_final, w_out_bf, w1_bf, w2_bf, 1, tt_p, final)
        kp_l.append(kp.reshape(bp, seq, N_DA_HEADS, dv))
        vp_l.append(vp.reshape(bp, seq, N_DA_HEADS, dv))
        cp_l.append(cp)
        hp_l.append(hp.reshape(bp, lru_w))

        y_lru, cs, hs = _lru_sample(xl_s, gl_s, state_conv[l], state_h[l], conv_w[l], conv_b[l],
                                    wg_bf, bg, lru_lambda[l], t_new)
        ys = _mlp(ys, o_s.reshape(bs * t_new, da_w), y_lru, mods_s[2], mods_s[4], mods_s[3],
                  mods_s[5], g_norm2[l], g_final, w_out_bf, w1_bf, w2_bf, bb_s, t_new, final)
        ks_l.append(ks.reshape(bs, t_new, N_DA_HEADS, dv))
        vs_l.append(vs.reshape(bs, t_new, N_DA_HEADS, dv))
        cs_l.append(cs.reshape(bs, hist, lru_w))
        hs_l.append(hs)

    return (yp, ys,
            jnp.stack(kp_l), jnp.stack(vp_l), jnp.stack(cp_l), jnp.stack(hp_l),
            jnp.stack(ks_l), jnp.stack(vs_l), jnp.stack(cs_l), jnp.stack(hs_l))
```

```python
import functools
import math

import jax
import jax.numpy as jnp
import numpy as np
from jax import lax
from jax.experimental import pallas as pl
from jax.experimental.pallas import tpu as pltpu

F32 = jnp.float32
BF16 = jnp.bfloat16

N_DA_HEADS = 4
N_LRU_HEADS = 8
CONV_WIDTH = 4
LRU_C = 8.0
EPS = 1e-6
NEG_BIG = -1e30
SUBLANES = 8
LANES = 128
MXU_DIM = 256
VMEM_LIMIT_BYTES = 56 * 1024 * 1024
SAMPLE_RING_SLOTS = 4
LRU_SEGMENTS = SUBLANES
LRU_SEG_PAD = SUBLANES


def _lambda_init(layer):
    return 0.8 - 0.6 * math.exp(-0.3 * layer)


def _params(*sem):
    return pltpu.CompilerParams(dimension_semantics=sem, vmem_limit_bytes=VMEM_LIMIT_BYTES)


def _rms(x, g):
    return x * lax.rsqrt(jnp.mean(x * x, axis=-1, keepdims=True) + EPS) * g


def _ada_kernel(c_ref, w_ref, b_ref, o_ref):
    c = c_ref[...]
    s = (c * jax.nn.sigmoid(c)).astype(BF16)
    o_ref[...] = jnp.dot(s, w_ref[...].astype(BF16), preferred_element_type=F32) + b_ref[...]


def _ada_mod(c, w, b, tn=1536):
    m, d = c.shape
    n = w.shape[1]
    return pl.pallas_call(
        _ada_kernel,
        grid=(n // tn,),
        in_specs=[pl.BlockSpec((m, d), lambda j: (0, 0)),
                  pl.BlockSpec((d, tn), lambda j: (0, j)),
                  pl.BlockSpec((1, tn), lambda j: (0, j))],
        out_specs=pl.BlockSpec((m, tn), lambda j: (0, j)),
        out_shape=jax.ShapeDtypeStruct((m, n), F32),
        compiler_params=_params("arbitrary"),
        name="ada_mod",
    )(c, w, b.reshape(1, n))


def _inproj_kernel(x_ref, sc_ref, sh_ref, g_ref, w_ref, *refs, da_w, lru_w, q_scale, tk):
    bb, tt, d = x_ref.shape
    rows = bb * tt
    dv = da_w // N_DA_HEADS
    h = _rms(x_ref[...], g_ref[...]) * (1.0 + sc_ref[...]) + sh_ref[...]
    h = h.reshape(rows, d).astype(BF16)
    proj = jnp.dot(h, w_ref[...], preferred_element_type=F32)
    q = proj[:, :da_w] * q_scale
    k = proj[:, da_w:2 * da_w]
    v = proj[:, 2 * da_w:3 * da_w]
    if tk is None:
        q_ref, k4_ref, v4_ref, xl_ref, gl_ref = refs
        q_ref[...] = q.astype(BF16)
    else:
        qt_ref, k4_ref, v4_ref, kb_ref, vt_ref, xl_ref, gl_ref = refs
        qt_ref[...] = q.T.astype(BF16)
        kb_ref[...] = k.astype(BF16)
        vt = v.T.astype(BF16)
        for c in range(rows // tk):
            vt_ref[c] = vt[:, c * tk:(c + 1) * tk]
    for hh in range(N_DA_HEADS):
        k4_ref[pl.ds(hh, rows, stride=N_DA_HEADS), :] = k[:, hh * dv:(hh + 1) * dv]
        v4_ref[pl.ds(hh, rows, stride=N_DA_HEADS), :] = v[:, hh * dv:(hh + 1) * dv]
    xl_ref[...] = proj[:, 3 * da_w:3 * da_w + lru_w]
    gl_ref[...] = proj[:, 3 * da_w + lru_w:]


def _in_proj(x, sc, sh, g, w_bf, bb, tt, da_w, lru_w, tk=None):
    b, t, d = x.shape
    n_tok = b * t
    nt = t // tt
    rows = bb * tt
    dv = da_w // N_DA_HEADS
    dqk = dv // 2

    def flat(width, dtype, mult=1):
        return (pl.BlockSpec((rows * mult, width), lambda i, j: (i * nt + j, 0)),
                jax.ShapeDtypeStruct((n_tok * mult, width), dtype))

    kv4 = [flat(dv, F32, N_DA_HEADS), flat(dv, F32, N_DA_HEADS)]
    lru = [flat(lru_w, F32), flat(lru_w, F32)]
    if tk is None:
        outs = [flat(da_w, BF16)] + kv4 + lru
    else:
        assert bb == 1 and tt % tk == 0
        qt = (pl.BlockSpec((None, da_w, tt), lambda i, j: (i, 0, j)),
              jax.ShapeDtypeStruct((b, da_w, t), BF16))
        vt = (pl.BlockSpec((None, tt // tk, da_w, tk), lambda i, j: (i, j, 0, 0)),
              jax.ShapeDtypeStruct((b, t // tk, da_w, tk), BF16))
        outs = [qt] + kv4 + [flat(da_w, BF16), vt] + lru
    return pl.pallas_call(
        functools.partial(_inproj_kernel, da_w=da_w, lru_w=lru_w, q_scale=dqk ** -0.5, tk=tk),
        grid=(b // bb, nt),
        in_specs=[pl.BlockSpec((bb, tt, d), lambda i, j: (i, j, 0)),
                  pl.BlockSpec((bb, 1, d), lambda i, j: (i, 0, 0)),
                  pl.BlockSpec((bb, 1, d), lambda i, j: (i, 0, 0)),
                  pl.BlockSpec((1, 1, d), lambda i, j: (0, 0, 0)),
                  pl.BlockSpec(w_bf.shape, lambda i, j: (0, 0))],
        out_specs=[o[0] for o in outs],
        out_shape=[o[1] for o in outs],
        compiler_params=_params("arbitrary", "arbitrary"),
        name="in_proj",
    )(x, sc, sh, g.reshape(1, 1, d), w_bf)


def _lam_value(lamp_ref, lam0):
    lp = lamp_ref[...]
    t1 = jnp.sum(lp[0:1] * lp[1:2], axis=-1, keepdims=True)
    t2 = jnp.sum(lp[2:3] * lp[3:4], axis=-1, keepdims=True)
    return jnp.exp(t1) - jnp.exp(t2) + lam0


def _softmax_update(s, m_sc, l_sc, acc_sc, pv_fn):
    m_old = m_sc[...]
    m_new = jnp.maximum(m_old, jnp.max(s, axis=-1, keepdims=True))
    alpha = jnp.exp(m_old - m_new)
    p = jnp.exp(s - m_new)
    l_sc[...] = alpha * l_sc[...] + jnp.sum(p, axis=-1, keepdims=True)
    acc_sc[...] = alpha * acc_sc[...] + pv_fn(p.astype(BF16))
    m_sc[...] = m_new


def _nt_dot(a, b):
    return lax.dot_general(a, b, (((1,), (1,)), ((), ())), preferred_element_type=F32)


(WK_ATT_RUN, WK_ATT_B, WK_ATT_QI, WK_ATT_J, WK_ATT_OUT_B, WK_ATT_OUT_QI, WORK_COLS) = range(7)


def _prompt_work_table(n_steps, n_batch, n_qt):
    units = [(b, qi, j) for b in range(n_batch) for qi in range(n_qt) for j in range(qi + 1)]
    assert len(units) <= n_steps, "prompt attention units must fit the sample-attention grid"
    step_of = {u * n_steps // len(units): u for u in range(len(units))}
    table = np.zeros((n_steps, WORK_COLS), np.int32)
    nxt, done = 0, 0
    for g in range(n_steps):
        u = step_of.get(g)
        if u is not None:
            nxt, done = u + 1, u
        b_in, qi_in, j_in = units[u if u is not None else min(nxt, len(units) - 1)]
        table[g] = (u is not None, b_in, qi_in, j_in) + units[done][:2]
    return table.reshape(-1)


def _prompt_unit(qi, j, slopes_ref, qt_ref, k_ref, vt_ref, lamp_ref, g_ref, o_ref,
                 m_sc, l_sc, acc_sc, s_sc, qst_sc, *, tq, lam0):
    dv = g_ref.shape[0]
    heads = range(N_DA_HEADS)
    cols = [slice(h * dv, (h + 1) * dv) for h in heads]
    key = lax.broadcasted_iota(jnp.int32, (tq, LANES), 0).astype(F32)

    def scores(jj, slot):
        start = pl.multiple_of(jj * tq, tq)
        for h in heads:
            s_sc[slot, h] = jnp.dot(k_ref[pl.ds(start, tq), cols[h]], qst_sc[h],
                                    preferred_element_type=F32)

    def update(slot, masked):
        if masked:
            kk = lax.broadcasted_iota(jnp.int32, (tq, 2 * tq), 0)
            qq = lax.broadcasted_iota(jnp.int32, (tq, 2 * tq), 1)
            visible = kk <= jnp.where(qq >= tq, qq - tq, qq)
        tile_off = ((j - qi) * tq).astype(F32)
        for h in heads:
            s = s_sc[slot, h] + jnp.concatenate([slopes_ref[h] * key] * (2 * tq // LANES), axis=1)
            if masked:
                s = jnp.where(visible, s, NEG_BIG)
            shift = slopes_ref[h] * tile_off
            m_old = m_sc[h]
            m_new = jnp.maximum(m_old, jnp.max(s, axis=0, keepdims=True) + shift)
            alpha = jnp.exp(m_old - m_new)
            p = jnp.exp(s - (m_new - shift))
            l_sc[h] = alpha * l_sc[h] + jnp.sum(p, axis=0, keepdims=True)
            acc_sc[h] = alpha * acc_sc[h] + jnp.dot(vt_ref[j, cols[h], :], p.astype(BF16),
                                                    preferred_element_type=F32)
            m_sc[h] = m_new

    @pl.when(j == 0)
    def _start():
        row = lax.broadcasted_iota(jnp.int32, (dv, tq), 0)
        zero = jnp.zeros((dv, tq), BF16)
        for h in heads:
            qt = qt_ref[cols[h], :]
            qst_sc[h] = jnp.concatenate([jnp.where(row < dv // 2, qt, zero),
                                         jnp.where(row >= dv // 2, qt, zero)], axis=1)
        m_sc[...] = jnp.full(m_sc.shape, NEG_BIG, F32)
        l_sc[...] = jnp.zeros(l_sc.shape, F32)
        acc_sc[...] = jnp.zeros(acc_sc.shape, F32)
        scores(0, 0)

    for parity in (0, 1):
        @pl.when((j < qi) & (j % 2 == parity))
        def _off_diagonal():
            scores(j + 1, 1 - parity)
            update(parity, False)

        @pl.when((j == qi) & (j % 2 == parity))
        def _diagonal():
            update(parity, True)

    @pl.when(j == qi)
    def _finish():
        lam = _lam_value(lamp_ref, lam0)
        outs = []
        for h in heads:
            on = acc_sc[h] * (1.0 / l_sc[h])
            ot = on[:, :tq] - lam * on[:, tq:]
            ms = jnp.mean(ot * ot, axis=0, keepdims=True)
            outs.append((ot * lax.rsqrt(ms + EPS) * g_ref[...] * (1.0 - lam0)).T)
        o_ref[...] = jnp.concatenate(outs, axis=1).astype(o_ref.dtype)


def _attn_kernel(pt_ref, slopes_ref, work_ref,
                 q_ref, kn_ref, vn_ref, lamp_ref, g_ref, kc_hbm, vc_hbm,
                 qt_ref, kp_ref, vt_ref, gcol_ref,
                 o_ref, op_ref,
                 kbuf, vbuf, sem, m_sc, l_sc, acc_sc,
                 pm_sc, pl_sc, pacc_sc, ps_sc, qst_sc,
                 *, n_seq, nj, n_pg, pg_rows, past, lam0, pool_off, tq):
    seq = pl.program_id(0)
    j = pl.program_id(1)
    n_chunks = n_seq * nj
    n_slots = kbuf.shape[0]
    g = seq * nj + j
    t_new, da_w = q_ref.shape
    dv = da_w // N_DA_HEADS
    rows_h = 2 * t_new
    n_rows = N_DA_HEADS * rows_h
    n_keys = n_pg * pg_rows // N_DA_HEADS

    def chunk_copies(chunk, slot):
        cs = chunk // nj
        cj = chunk % nj
        copies = []
        for i in range(n_pg):
            page = pt_ref[cs, cj * n_pg + i] + pool_off
            rows = pl.ds(i * pg_rows, pg_rows)
            copies.append(pltpu.make_async_copy(kc_hbm.at[page], kbuf.at[slot, rows], sem.at[0, slot]))
            copies.append(pltpu.make_async_copy(vc_hbm.at[page], vbuf.at[slot, rows], sem.at[1, slot]))
        return copies

    @pl.when(g == 0)
    def _prologue():
        for c in range(min(n_slots - 1, n_chunks)):
            for cp in chunk_copies(c, c):
                cp.start()

    ahead = g + (n_slots - 1)

    @pl.when(ahead < n_chunks)
    def _prefetch():
        for cp in chunk_copies(ahead, ahead % n_slots):
            cp.start()

    work = lambda col: work_ref[g * WORK_COLS + col]

    @pl.when(work(WK_ATT_RUN) == 1)
    def _prompt_attention():
        _prompt_unit(work(WK_ATT_QI), work(WK_ATT_J), slopes_ref,
                     qt_ref, kp_ref, vt_ref, lamp_ref, gcol_ref, op_ref,
                     pm_sc, pl_sc, pacc_sc, ps_sc, qst_sc, tq=tq, lam0=lam0)

    slot = g % n_slots
    for cp in chunk_copies(g, slot):
        cp.wait()

    row1 = lax.broadcasted_iota(jnp.int32, (n_rows, 1), 0)
    h_row = row1 // rows_h
    q_row = row1 % t_new
    slope_row = jnp.exp2(-8.0 * (h_row + 1).astype(F32) / N_DA_HEADS)

    qf = q_ref[...].astype(F32)
    lane = lax.broadcasted_iota(jnp.int32, (t_new, dv), 1)
    pieces = []
    for h in range(N_DA_HEADS):
        qh = qf[:, h * dv:(h + 1) * dv]
        pieces.append(jnp.where(lane < dv // 2, qh, 0.0))
        pieces.append(jnp.where(lane >= dv // 2, qh, 0.0))
    qall = jnp.concatenate(pieces, axis=0).astype(BF16)

    @pl.when(j == 0)
    def _init():
        m_sc[...] = jnp.full(m_sc.shape, NEG_BIG, F32)
        l_sc[...] = jnp.zeros(l_sc.shape, F32)
        acc_sc[...] = jnp.zeros(acc_sc.shape, F32)

    grp = MXU_DIM // dv
    n_grp = N_DA_HEADS // grp

    def group_rows(buf, gi):
        return jnp.concatenate([buf[slot, pl.ds(gi * grp + hh, n_keys, stride=N_DA_HEADS), :]
                                for hh in range(grp)], axis=1).astype(BF16)

    zq = jnp.zeros((rows_h, dv), F32)
    s_parts = []
    for gi in range(n_grp):
        qg = jnp.concatenate(
            [jnp.concatenate([jnp.concatenate(pieces[2 * (gi * grp + hh):2 * (gi * grp + hh) + 2], axis=0)
                              if cc == hh else zq for cc in range(grp)], axis=1)
             for hh in range(grp)], axis=0).astype(BF16)
        s_parts.append(_nt_dot(qg, group_rows(kbuf, gi)))
    s = jnp.concatenate(s_parts, axis=0)
    kpos = lax.broadcasted_iota(jnp.int32, (1, n_keys), 1) + (j * n_keys - past)
    s = s + slope_row * kpos.astype(F32)

    def pv(p):
        outs = []
        for gi in range(n_grp):
            og = jnp.dot(p[gi * grp * rows_h:(gi + 1) * grp * rows_h], group_rows(vbuf, gi),
                         preferred_element_type=F32)
            outs += [og[hh * rows_h:(hh + 1) * rows_h, hh * dv:(hh + 1) * dv] for hh in range(grp)]
        return jnp.concatenate(outs, axis=0)

    _softmax_update(s, m_sc, l_sc, acc_sc, pv)

    @pl.when(j == nj - 1)
    def _finish():
        n_new = kn_ref.shape[0]
        pad = jnp.zeros((LANES - n_new, dv), F32)
        kn = jnp.concatenate([kn_ref[...], pad], axis=0).astype(BF16)
        vn = jnp.concatenate([vn_ref[...], pad], axis=0).astype(BF16)
        c = lax.broadcasted_iota(jnp.int32, (n_rows, LANES), 1)
        key = c // N_DA_HEADS
        ok = (c % N_DA_HEADS == h_row) & (key <= q_row) & (c < n_new)
        sn = jnp.where(ok, _nt_dot(qall, kn) + slope_row * key.astype(F32), NEG_BIG)
        _softmax_update(sn, m_sc, l_sc, acc_sc,
                        lambda p: jnp.dot(p, vn, preferred_element_type=F32))

        lam = _lam_value(lamp_ref, lam0)
        on = acc_sc[...] / l_sc[...]
        outs = []
        for h in range(N_DA_HEADS):
            o1 = on[h * rows_h:h * rows_h + t_new]
            o2 = on[h * rows_h + t_new:(h + 1) * rows_h]
            outs.append(_rms(o1 - lam * o2, g_ref[...]) * (1.0 - lam0))
        o_ref[...] = jnp.concatenate(outs, axis=1).astype(o_ref.dtype)


def _attention(q, k_new, v_new, cache_k2, cache_v2, page_table, pool_off, qt, kb, vt, lamp, g_subln,
               slopes, lam0, n_pg, tq):
    b, t_new, da_w = q.shape
    bp, s, _ = kb.shape
    dv = da_w // N_DA_HEADS
    n_pages = page_table.shape[1]
    pg_rows = cache_k2.shape[1]
    past = n_pages * (pg_rows // N_DA_HEADS)
    n_rows = N_DA_HEADS * 2 * t_new
    nj = n_pages // n_pg
    work = jnp.asarray(_prompt_work_table(b * nj, bp, s // tq))
    kernel = functools.partial(_attn_kernel, n_seq=b, nj=nj, n_pg=n_pg, pg_rows=pg_rows, past=past,
                               lam0=lam0, pool_off=pool_off, tq=tq)

    def wk(col):
        return lambda i, j, pt, sl, w: w[(i * nj + j) * WORK_COLS + col]

    b_in, qi_in, b_out, qi_out = wk(WK_ATT_B), wk(WK_ATT_QI), wk(WK_ATT_OUT_B), wk(WK_ATT_OUT_QI)
    seq_blk = lambda i, j, pt, sl, w: (i, 0, 0)
    const2 = lambda i, j, pt, sl, w: (0, 0)
    in_specs = [pl.BlockSpec((None, t_new, da_w), seq_blk),
                pl.BlockSpec((None, t_new * N_DA_HEADS, dv), seq_blk),
                pl.BlockSpec((None, t_new * N_DA_HEADS, dv), seq_blk),
                pl.BlockSpec(lamp.shape, const2),
                pl.BlockSpec((1, dv), const2),
                pl.BlockSpec(memory_space=pl.ANY),
                pl.BlockSpec(memory_space=pl.ANY),
                pl.BlockSpec((None, da_w, tq), lambda *a: (b_in(*a), 0, qi_in(*a))),
                pl.BlockSpec((None, s, da_w), lambda *a: (b_in(*a), 0, 0)),
                pl.BlockSpec((None, s // tq, da_w, tq), lambda *a: (b_in(*a), 0, 0, 0)),
                pl.BlockSpec((dv, 1), const2)]
    out_specs = [pl.BlockSpec((None, t_new, da_w), seq_blk),
                 pl.BlockSpec((None, tq, da_w), lambda *a: (b_out(*a), qi_out(*a), 0))]
    ring = pltpu.VMEM((SAMPLE_RING_SLOTS, n_pg * pg_rows, dv), F32)
    return pl.pallas_call(
        kernel,
        grid_spec=pltpu.PrefetchScalarGridSpec(
            num_scalar_prefetch=3,
            grid=(b, nj),
            in_specs=in_specs,
            out_specs=out_specs,
            scratch_shapes=[ring, ring, pltpu.SemaphoreType.DMA((2, SAMPLE_RING_SLOTS)),
                            pltpu.VMEM((n_rows, 1), F32), pltpu.VMEM((n_rows, 1), F32),
                            pltpu.VMEM((n_rows, dv), F32),
                            pltpu.VMEM((N_DA_HEADS, 1, 2 * tq), F32),
                            pltpu.VMEM((N_DA_HEADS, 1, 2 * tq), F32),
                            pltpu.VMEM((N_DA_HEADS, dv, 2 * tq), F32),
                            pltpu.VMEM((2, N_DA_HEADS, tq, 2 * tq), F32),
                            pltpu.VMEM((N_DA_HEADS, dv, 2 * tq), BF16)]),
        out_shape=[jax.ShapeDtypeStruct((b, t_new, da_w), BF16),
                   jax.ShapeDtypeStruct((bp, s, da_w), BF16)],
        compiler_params=_params("arbitrary", "arbitrary"),
        name="attention",
    )(page_table, slopes, work, q, k_new, v_new, lamp, g_subln.reshape(1, dv), cache_k2, cache_v2,
      qt, kb, vt, g_subln.reshape(dv, 1))


def _softplus(z):
    return jnp.maximum(z, 0.0) + jnp.log1p(jnp.exp(-jnp.abs(z)))


def _sigmoid(x):
    return 0.5 * jnp.tanh(0.5 * x) + 0.5


def _gelu_tanh(x):
    c = math.sqrt(2.0 / math.pi)
    return x * (0.5 + 0.5 * jnp.tanh(x * (c + (0.044715 * c) * (x * x))))


def _lru_gates(xc, wg_ref, bg_ref, lam_ref):
    w = xc.shape[-1]
    g = jnp.dot(xc.astype(BF16), wg_ref[...], preferred_element_type=F32) + bg_ref[...]
    r = _sigmoid(g[:, :w])
    ig = _sigmoid(g[:, w:])
    log_a = r * (-LRU_C * _softplus(-lam_ref[...]))
    a = jnp.exp(log_a)
    u = jnp.sqrt(-jnp.tanh(log_a) * (1.0 + a * a)) * (ig * xc)
    return a, u


def _load_blocked(sc, rows):
    return jnp.concatenate([sc[c, rows, :] for c in range(sc.shape[0])], axis=1)


def _store_blocked(sc, rows, val):
    for c in range(sc.shape[0]):
        sc[c, rows, :] = val[:, c * LANES:(c + 1) * LANES]


def _blocked(rows, w):
    return pltpu.VMEM((w // LANES, rows, LANES), F32)


def _lru_prompt_kernel(xl_ref, gl_ref, cw_ref, cb_ref, wg_ref, bg_ref, lam_ref,
                       y_ref, conv_ref, hlast_ref, xbuf, a_sc, u_sc, hcar):
    t = pl.program_id(1)
    nt = pl.num_programs(1)
    tm, w = xl_ref.shape
    seg = tm // LRU_SEGMENTS
    pitch = seg + LRU_SEG_PAD
    hist = CONV_WIDTH - 1

    @pl.when(t == 0)
    def _init():
        xbuf[0:SUBLANES, :] = jnp.zeros((SUBLANES, w), F32)
        hcar[...] = jnp.zeros(hcar.shape, F32)

    xbuf[SUBLANES:SUBLANES + tm, :] = xl_ref[...]
    xall = xbuf[...]
    xc = cb_ref[...]
    for jj in range(CONV_WIDTH):
        shifted = xall if jj == hist else pltpu.roll(xall, hist - jj, axis=0)
        xc = xc + shifted[SUBLANES:, :] * cw_ref[jj:jj + 1, :]
    a, u = _lru_gates(xc, wg_ref, bg_ref, lam_ref)
    for s in range(LRU_SEGMENTS):
        _store_blocked(a_sc, slice(s * pitch, s * pitch + seg), a[s * seg:(s + 1) * seg])
        _store_blocked(u_sc, slice(s * pitch, s * pitch + seg), u[s * seg:(s + 1) * seg])

    def step(i, carry):
        p, hh = carry
        rows = pl.ds(i, LRU_SEGMENTS, stride=pitch)
        ai = _load_blocked(a_sc, rows)
        p = ai * p
        hh = ai * hh + _load_blocked(u_sc, rows)
        _store_blocked(a_sc, rows, p)
        _store_blocked(u_sc, rows, hh)
        return p, hh

    p_end, h_end = lax.fori_loop(0, seg, step,
                                 (jnp.ones((LRU_SEGMENTS, w), F32), jnp.zeros((LRU_SEGMENTS, w), F32)))

    h_in = hcar[...]
    for s in range(LRU_SEGMENTS):
        blk = slice(s * seg, (s + 1) * seg)
        sblk = slice(s * pitch, s * pitch + seg)
        hs = _load_blocked(u_sc, sblk) + _load_blocked(a_sc, sblk) * h_in
        y_ref[blk, :] = (hs * _gelu_tanh(gl_ref[blk, :])).astype(y_ref.dtype)
        h_in = p_end[s:s + 1, :] * h_in + h_end[s:s + 1, :]
    hcar[...] = h_in
    xbuf[0:SUBLANES, :] = xbuf[tm:tm + SUBLANES, :]

    @pl.when(t == nt - 1)
    def _fin():
        hlast_ref[...] = h_in
        conv_ref[...] = xbuf[pl.ds(SUBLANES - hist, hist), :]


def _lru_prompt(xl, gl, conv_w, conv_b, wg_bf, bg, lru_lambda, b, s, tm):
    w = xl.shape[-1]
    nt = s // tm
    hist = CONV_WIDTH - 1
    scan_rows = LRU_SEGMENTS * (tm // LRU_SEGMENTS + LRU_SEG_PAD)
    full = lambda shape: pl.BlockSpec(shape, lambda i, j: (0,) * len(shape))
    return pl.pallas_call(
        _lru_prompt_kernel,
        grid=(b, nt),
        in_specs=[pl.BlockSpec((tm, w), lambda i, j: (i * nt + j, 0)),
                  pl.BlockSpec((tm, w), lambda i, j: (i * nt + j, 0)),
                  full((CONV_WIDTH, w)), full((1, w)), full(wg_bf.shape), full((1, 2 * w)),
                  full((1, w))],
        out_specs=[pl.BlockSpec((tm, w), lambda i, j: (i * nt + j, 0)),
                   pl.BlockSpec((None, hist, w), lambda i, j: (i, 0, 0)),
                   pl.BlockSpec((None, 1, w), lambda i, j: (i, 0, 0))],
        out_shape=[jax.ShapeDtypeStruct((b * s, w), BF16),
                   jax.ShapeDtypeStruct((b, hist, w), F32),
                   jax.ShapeDtypeStruct((b, 1, w), F32)],
        scratch_shapes=[pltpu.VMEM((tm + SUBLANES, w), F32), _blocked(scan_rows, w),
                        _blocked(scan_rows, w), pltpu.VMEM((1, w), F32)],
        compiler_params=_params("arbitrary", "arbitrary"),
        name="lru_prompt",
    )(xl, gl, conv_w, conv_b.reshape(1, w), wg_bf, bg.reshape(1, 2 * w), lru_lambda.reshape(1, w))


def _lru_sample_kernel(xl_ref, gl_ref, cbuf_ref, h0_ref, cw_ref, cb_ref, wg_ref, bg_ref, lam_ref,
                       y_ref, conv_ref, hlast_ref, x_sc, g_sc, c_sc, y_sc, *, t_new):
    nb = h0_ref.shape[0]
    hist = CONV_WIDTH - 1
    _store_blocked(x_sc, slice(None), xl_ref[...])
    _store_blocked(g_sc, slice(None), gl_ref[...])
    _store_blocked(c_sc, slice(None), cbuf_ref[...])
    xp = [_load_blocked(c_sc, pl.ds(jj, nb, stride=hist)) for jj in range(hist)]
    xp += [_load_blocked(x_sc, pl.ds(tt, nb, stride=t_new)) for tt in range(t_new)]
    hh = h0_ref[...]
    for tt in range(t_new):
        xc = cb_ref[...]
        for jj in range(CONV_WIDTH):
            xc = xc + xp[tt + jj] * cw_ref[jj:jj + 1, :]
        a, u = _lru_gates(xc, wg_ref, bg_ref, lam_ref)
        hh = a * hh + u
        gate = _gelu_tanh(_load_blocked(g_sc, pl.ds(tt, nb, stride=t_new)))
        _store_blocked(y_sc, pl.ds(tt, nb, stride=t_new), hh * gate)
    hlast_ref[...] = hh
    y_ref[...] = _load_blocked(y_sc, slice(None)).astype(y_ref.dtype)
    for jj in range(hist):
        _store_blocked(c_sc, pl.ds(jj, nb, stride=hist), xp[t_new + jj])
    conv_ref[...] = _load_blocked(c_sc, slice(None))


def _lru_sample(xl, gl, conv_buf, h0, conv_w, conv_b, wg_bf, bg, lru_lambda, t_new):
    n_tok, w = xl.shape
    nb = n_tok // t_new
    hist = CONV_WIDTH - 1
    full = lambda shape: pl.BlockSpec(shape, lambda i: (0,) * len(shape))
    return pl.pallas_call(
        functools.partial(_lru_sample_kernel, t_new=t_new),
        grid=(1,),
        in_specs=[full((n_tok, w)), full((n_tok, w)), full((nb * hist, w)), full((nb, w)),
                  full((CONV_WIDTH, w)), full((1, w)), full(wg_bf.shape), full((1, 2 * w)),
                  full((1, w))],
        out_specs=[full((n_tok, w)), full((nb * hist, w)), full((nb, w))],
        out_shape=[jax.ShapeDtypeStruct((n_tok, w), BF16),
                   jax.ShapeDtypeStruct((nb * hist, w), F32),
                   jax.ShapeDtypeStruct((nb, w), F32)],
        scratch_shapes=[_blocked(n_tok, w), _blocked(n_tok, w), _blocked(nb * hist, w),
                        _blocked(n_tok, w)],
        compiler_params=_params("arbitrary"),
        name="lru_sample",
    )(xl, gl, conv_buf.reshape(nb * hist, w), h0, conv_w, conv_b.reshape(1, w), wg_bf,
      bg.reshape(1, 2 * w), lru_lambda.reshape(1, w))


def _mlp_kernel(x_ref, o_ref, y_ref, gt1_ref, sc2_ref, sh2_ref, gt2_ref, g2_ref, gf_ref,
                wout_ref, w1_ref, w2_ref, out_ref, *, ff_chunk, final_norm):
    bb, tt, d = x_ref.shape
    rows = bb * tt
    mix_in = jnp.concatenate([o_ref[...], y_ref[...]], axis=1)
    mix = jnp.dot(mix_in, wout_ref[...], preferred_element_type=F32).reshape(bb, tt, d)
    x1 = x_ref[...] + gt1_ref[...] * mix
    h2 = (_rms(x1, g2_ref[...]) * (1.0 + sc2_ref[...]) + sh2_ref[...]).reshape(rows, d).astype(BF16)
    d_ff = w1_ref.shape[1]
    ff = jnp.zeros((rows, d), F32)
    for c in range(d_ff // ff_chunk):
        cs = slice(c * ff_chunk, (c + 1) * ff_chunk)
        hc = jnp.dot(h2, w1_ref[:, cs], preferred_element_type=F32)
        hc = jnp.square(jnp.maximum(hc, 0.0)).astype(BF16)
        ff = ff + jnp.dot(hc, w2_ref[cs, :], preferred_element_type=F32)
    x2 = x1 + gt2_ref[...] * ff.reshape(bb, tt, d)
    out_ref[...] = _rms(x2, gf_ref[...]) if final_norm else x2


def _mlp(x, o, y, gt1, sc2, sh2, gt2, g2, gf, wout_bf, w1_bf, w2_bf, bb, tt, final_norm):
    b, t, d = x.shape
    nt = t // tt
    rows = bb * tt
    mix_w = o.shape[-1]
    mod = pl.BlockSpec((bb, 1, d), lambda i, j: (i, 0, 0))
    gain = pl.BlockSpec((1, 1, d), lambda i, j: (0, 0, 0))
    wspec = lambda w: pl.BlockSpec(w.shape, lambda i, j: (0, 0), pipeline_mode=pl.Buffered(1))
    return pl.pallas_call(
        functools.partial(_mlp_kernel, ff_chunk=min(1024, w1_bf.shape[1]), final_norm=final_norm),
        grid=(b // bb, nt),
        in_specs=[pl.BlockSpec((bb, tt, d), lambda i, j: (i, j, 0)),
                  pl.BlockSpec((rows, mix_w), lambda i, j: (i * nt + j, 0)),
                  pl.BlockSpec((rows, y.shape[-1]), lambda i, j: (i * nt + j, 0)),
                  mod, mod, mod, mod, gain, gain, wspec(wout_bf), wspec(w1_bf), wspec(w2_bf)],
        out_specs=pl.BlockSpec((bb, tt, d), lambda i, j: (i, j, 0)),
        out_shape=jax.ShapeDtypeStruct((b, t, d), F32),
        compiler_params=_params("arbitrary", "arbitrary"),
        name="out_mlp",
    )(x, o, y, gt1, sc2, sh2, gt2, g2.reshape(1, 1, d), gf.reshape(1, 1, d), wout_bf, w1_bf, w2_bf)


def _block_diag(w):
    h, i, j = w.shape
    eye = jnp.eye(h, dtype=w.dtype)
    return (eye[:, None, :, None] * w[:, :, None, :]).reshape(h * i, h * j)


def _pick(n, target):
    t = min(n, target)
    while n % t:
        t -= 1
    return t


def kernel(x_prompt, x_sample, c_prompt, c_sample, cache_k, cache_v, page_table, state_h, state_conv, w_ada, b_ada, g_norm1, g_norm2, w_in, lambda_q1, lambda_k1, lambda_q2, lambda_k2, g_subln, conv_w, conv_b, w_rg, b_rg, w_ig, b_ig, lru_lambda, w_out, w_ff1, w_ff2, g_final):
    depth = w_in.shape[0]
    bp, seq, d = x_prompt.shape
    bs, t_new, _ = x_sample.shape
    lru_w = conv_w.shape[-1]
    da_w = (w_in.shape[-1] - 2 * lru_w) // 3
    dv = da_w // N_DA_HEADS
    n_pool, page_size = cache_k.shape[1], cache_k.shape[2]
    n_pages = page_table.shape[1]
    hist = CONV_WIDTH - 1

    slopes = 2.0 ** (-8.0 * jnp.arange(1, N_DA_HEADS + 1, dtype=F32) / N_DA_HEADS)
    cache_k2 = cache_k.reshape(depth * n_pool, page_size * N_DA_HEADS, dv)
    cache_v2 = cache_v.reshape(depth * n_pool, page_size * N_DA_HEADS, dv)

    tt_p = _pick(seq, 512)
    tt_in = _pick(seq, 1024)
    bb_s = _pick(bs, 64)
    tq = _pick(seq, 256)
    n_pg = _pick(n_pages, 16)
    tm_lru = _pick(seq, 512)

    yp, ys = x_prompt, x_sample
    kp_l, vp_l, cp_l, hp_l, ks_l, vs_l, cs_l, hs_l = [], [], [], [], [], [], [], []
    for l in range(depth):
        lam0 = _lambda_init(l)
        w_in_bf = w_in[l].astype(BF16)
        w_out_bf = w_out[l].astype(BF16)
        w1_bf = w_ff1[l].astype(BF16)
        w2_bf = w_ff2[l].astype(BF16)
        wg_bf = jnp.concatenate([_block_diag(w_rg[l]), _block_diag(w_ig[l])], axis=1).astype(BF16)
        bg = jnp.concatenate([b_rg[l], b_ig[l]])
        lamp = jnp.stack([lambda_q1[l], lambda_k1[l], lambda_q2[l], lambda_k2[l]])
        final = l == depth - 1

        mod = _ada_mod(jnp.concatenate([c_prompt, c_sample], axis=0), w_ada[l], b_ada[l])
        mods_p = [m[:, None, :] for m in jnp.split(mod[:bp], 6, axis=-1)]
        mods_s = [m[:, None, :] for m in jnp.split(mod[bp:], 6, axis=-1)]

        qt, kp, vp, kb, vt, xl_p, gl_p = _in_proj(yp, mods_p[1], mods_p[0], g_norm1[l], w_in_bf,
                                                  1, tt_in, da_w, lru_w, tk=tq)
        q, ks, vs, xl_s, gl_s = _in_proj(ys, mods_s[1], mods_s[0], g_norm1[l], w_in_bf,
                                         bb_s, t_new, da_w, lru_w)
        o_s, o_p = _attention(
            q.reshape(bs, t_new, da_w), ks.reshape(bs, t_new * N_DA_HEADS, dv),
            vs.reshape(bs, t_new * N_DA_HEADS, dv), cache_k2, cache_v2, page_table, l * n_pool,
            qt, kb.reshape(bp, seq, da_w), vt, lamp, g_subln[l], slopes, lam0, n_pg, tq)

        y_lru, cp, hp = _lru_prompt(xl_p, gl_p, conv_w[l], conv_b[l], wg_bf, bg, lru_lambda[l],
                                    bp, seq, tm_lru)
        yp = _mlp(yp, o_p.reshape(bp * seq, da_w), y_lru, mods_p[2], mods_p[4], mods_p[3],
                  mods_p[5], g_norm2[l], g_final, w_out_bf, w1_bf, w2_bf, 1, tt_p, final)
        kp_l.append(kp.reshape(bp, seq, N_DA_HEADS, dv))
        vp_l.append(vp.reshape(bp, seq, N_DA_HEADS, dv))
        cp_l.append(cp)
        hp_l.append(hp.reshape(bp, lru_w))

        y_lru, cs, hs = _lru_sample(xl_s, gl_s, state_conv[l], state_h[l], conv_w[l], conv_b[l],
                                    wg_bf, bg, lru_lambda[l], t_new)
        ys = _mlp(ys, o_s.reshape(bs * t_new, da_w), y_lru, mods_s[2], mods_s[4], mods_s[3],
                  mods_s[5], g_norm2[l], g_final, w_out_bf, w1_bf, w2_bf, bb_s, t_new, final)
        ks_l.append(ks.reshape(bs, t_new, N_DA_HEADS, dv))
        vs_l.append(vs.reshape(bs, t_new, N_DA_HEADS, dv))
        cs_l.append(cs.reshape(bs, hist, lru_w))
        hs_l.append(hs)

    return (yp, ys,
            jnp.stack(kp_l), jnp.stack(vp_l), jnp.stack(cp_l), jnp.stack(hp_l),
            jnp.stack(ks_l), jnp.stack(vs_l), jnp.stack(cs_l), jnp.stack(hs_l))
```
